```python
import math
import jax, jax.numpy as jnp
from jax import lax
import numpy as np

D_MODEL = 1024
BATCH = 32
SEQ = 256
DEPTH = 2
DEC_BATCH = 8
DEC_SEQ = 4096
PAST_LEN = 256

GRID_W = 64
EPS = 1e-6
ROPE_BASE = 10000.0
BLOCK = 128
DN_HEADS = 4
DN_DK = 128
DN_DV = 128
DN_CONV = 3
DN_CHUNK = 64
MLA_HEADS = 4
MLA_Q_LORA = 256
MLA_KV_LORA = 128
MLA_NOPE = 64
MLA_ROPE = 32
MLA_V = 64
SWA_HEADS = 4
SWA_KV_HEADS = 2
SWA_HD = 64
WINDOW = 128
D_FF = 4 * D_MODEL
N_MOD = 6

DN_CONV_CH = 2 * DN_HEADS * DN_DK + DN_HEADS * DN_DV
IN_SPLITS = (DN_CONV_CH, DN_HEADS * DN_DV, DN_HEADS, DN_HEADS, DN_HEADS, DN_HEADS,
             MLA_Q_LORA, MLA_KV_LORA, MLA_ROPE,
             SWA_HEADS * SWA_HD, SWA_KV_HEADS * SWA_HD, SWA_KV_HEADS * SWA_HD)
IN_W = sum(IN_SPLITS)
MIX_W = DN_HEADS * DN_DV + MLA_HEADS * MLA_V + SWA_HEADS * SWA_HD
MLA_SCALE = (MLA_NOPE + MLA_ROPE) ** -0.5
SWA_SCALE = SWA_HD ** -0.5

kernel_name = 'hybrid_flow_prefix_step'


def _rms_norm(x, g):
    xf = x.astype(jnp.float32)
    y = xf * lax.rsqrt(jnp.mean(xf * xf, axis=-1, keepdims=True) + EPS)
    return (y * g.astype(jnp.float32)).astype(x.dtype)


def _l2norm(x):
    xf = x.astype(jnp.float32)
    return xf * lax.rsqrt(jnp.sum(xf * xf, axis=-1, keepdims=True) + EPS)


def _axial_rope_tables(n_tok, rot_dim):
    rows = n_tok // GRID_W
    row = jnp.repeat(jnp.arange(rows, dtype=jnp.float32), GRID_W)
    col = jnp.tile(jnp.arange(GRID_W, dtype=jnp.float32), rows)
    n_freq = rot_dim // 4
    inv_freq = ROPE_BASE ** (-jnp.arange(n_freq, dtype=jnp.float32) / n_freq)
    ang = jnp.concatenate([row[:, None] * inv_freq, col[:, None] * inv_freq], axis=-1)
    return jnp.cos(ang), jnp.sin(ang)


def _apply_rope(x, cos, sin):
    xf = x.astype(jnp.float32)
    x1, x2 = jnp.split(xf, 2, axis=-1)
    c = cos[:, None, :]
    s = sin[:, None, :]
    return jnp.concatenate([x1 * c - x2 * s, x1 * s + x2 * c], axis=-1).astype(x.dtype)


def _short_conv(x, w):
    y = lax.conv_general_dilated(x, w[:, None, :].astype(x.dtype), window_strides=(1,),
                                 padding=[(DN_CONV // 2, DN_CONV // 2)],
                                 dimension_numbers=('NWC', 'WIO', 'NWC'),
                                 feature_group_count=x.shape[-1])
    return jax.nn.silu(y)


def _gdn_chunked(q, k, v, g, beta, s0):
    B, T, H, DK = q.shape
    DV = v.shape[-1]
    C = DN_CHUNK
    N = T // C
    f32 = jnp.float32

    def chunks(t):
        t = t.astype(f32).reshape((B, N, C, H) + t.shape[3:])
        return jnp.moveaxis(t, (1, 3), (0, 2))

    qc, kc, vc = chunks(q), chunks(k), chunks(v)
    gc = jnp.cumsum(chunks(g), axis=-1)
    bc = chunks(beta)
    idx = jnp.arange(C)
    incl = idx[:, None] >= idx[None, :]
    strict = idx[:, None] > idx[None, :]
    decay = jnp.exp(jnp.where(incl, gc[..., :, None] - gc[..., None, :], -jnp.inf))
    kb = kc * bc[..., None]
    lower = jnp.where(strict, jnp.einsum('nbhik,nbhjk->nbhij', kb, kc) * decay, 0.0)
    eye = jnp.eye(C, dtype=f32)
    rhs = jnp.concatenate([vc * bc[..., None], kb * jnp.exp(gc)[..., None]], axis=-1)
    sol = lax.linalg.triangular_solve(eye + lower, rhs, left_side=True, lower=True, unit_diagonal=True)
    u, w = sol[..., :DV], sol[..., DV:]
    intra = jnp.where(incl, jnp.einsum('nbhik,nbhjk->nbhij', qc, kc) * decay, 0.0)

    def step(S, inp):
        q_i, k_i, u_i, w_i, g_i, a_i = inp
        v_new = u_i - jnp.einsum('bhck,bhkv->bhcv', w_i, S)
        o_i = (jnp.einsum('bhck,bhkv->bhcv', q_i * jnp.exp(g_i)[..., None], S)
               + jnp.einsum('bhij,bhjv->bhiv', a_i, v_new))
        g_last = g_i[..., -1:]
        S = (S * jnp.exp(g_last)[..., None]
             + jnp.einsum('bhck,bhcv->bhkv', k_i * jnp.exp(g_last - g_i)[..., None], v_new))
        return S, o_i

    S, o = lax.scan(step, s0.astype(f32), (qc, kc, u, w, gc, intra))
    o = jnp.moveaxis(o, (0, 2), (1, 3)).reshape(B, T, H, DV)
    return o.astype(v.dtype), S


def _deltanet(q, k, v, z, a_f, a_b, b_f, b_b, a_log, dt_bias, norm_g, s0_f, s0_b):
    B, T, H, _ = q.shape
    qn = _l2norm(q) * (DN_DK ** -0.5)
    kn = _l2norm(k)
    f32 = jnp.float32
    g_f = -jnp.exp(a_log[0].astype(f32)) * jax.nn.softplus(a_f.astype(f32) + dt_bias[0].astype(f32))
    g_b = -jnp.exp(a_log[1].astype(f32)) * jax.nn.softplus(a_b.astype(f32) + dt_bias[1].astype(f32))
    beta_f = jax.nn.sigmoid(b_f.astype(f32))
    beta_b = jax.nn.sigmoid(b_b.astype(f32))
    o_f, s_f = _gdn_chunked(qn, kn, v, g_f, beta_f, s0_f)
    flip = lambda t: jnp.flip(t, axis=1)
    o_b, s_b = _gdn_chunked(flip(qn), flip(kn), flip(v), flip(g_b), flip(beta_b), s0_b)
    o = o_f + flip(o_b)
    o = _rms_norm(o, norm_g) * jax.nn.silu(z)
    return o.reshape(B, T, H * DN_DV), s_f, s_b


def _softmax_with_sink(s, sink):
    if sink is None:
        return jax.nn.softmax(s, axis=-1)
    sk = jnp.broadcast_to(sink.astype(jnp.float32)[:, :, None, None], s.shape[:-1] + (1,))
    return jax.nn.softmax(jnp.concatenate([s, sk], axis=-1), axis=-1)[..., :-1]


def _dense_attention(q, k, v, scale, sink):
    B, Tq, KH, G, D = q.shape
    nb = Tq // BLOCK
    qb = jnp.moveaxis(q.reshape(B, nb, BLOCK, KH, G, D), 1, 0)

    def one(q_i):
        s = jnp.einsum('bqkgd,btkd->bkgqt', q_i, k).astype(jnp.float32) * scale
        p = _softmax_with_sink(s, sink).astype(v.dtype)
        return jnp.einsum('bkgqt,btkv->bqkgv', p, v)

    o = lax.map(one, qb)
    return jnp.moveaxis(o, 0, 1).reshape(B, Tq, KH, G, v.shape[-1])


def _window_attention(q, k, v, k_ctx, v_ctx, sink, scale):
    B, T, KH, G, D = q.shape
    nb = T // BLOCK

    def band(a):
        ap = jnp.pad(a, ((0, 0), (BLOCK, BLOCK), (0, 0), (0, 0))).reshape(B, nb + 2, BLOCK, KH, a.shape[-1])
        return jnp.concatenate([ap[:, :nb], ap[:, 1:nb + 1], ap[:, 2:]], axis=2)

    qpos = jnp.arange(nb)[:, None, None] * BLOCK + jnp.arange(BLOCK)[None, :, None]
    kpos = (jnp.arange(nb)[:, None, None] - 1) * BLOCK + jnp.arange(3 * BLOCK)[None, None, :]
    valid = (jnp.abs(qpos - kpos) <= WINDOW) & (kpos >= 0) & (kpos < T)
    mv = lambda t: jnp.moveaxis(t, 1, 0)

    def one(args):
        q_i, k_i, v_i, m_i = args
        s_loc = jnp.einsum('bqkgd,btkd->bkgqt', q_i, k_i).astype(jnp.float32) * scale
        s_loc = jnp.where(m_i, s_loc, -jnp.inf)
        s_ctx = jnp.einsum('bqkgd,btkd->bkgqt', q_i, k_ctx).astype(jnp.float32) * scale
        p = _softmax_with_sink(jnp.concatenate([s_loc, s_ctx], axis=-1), sink).astype(v.dtype)
        return (jnp.einsum('bkgqt,btkv->bqkgv', p[..., :3 * BLOCK], v_i)
                + jnp.einsum('bkgqt,btkv->bqkgv', p[..., 3 * BLOCK:], v_ctx))

    o = lax.map(one, (mv(q.reshape(B, nb, BLOCK, KH, G, D)), mv(band(k)), mv(band(v)), valid))
    return jnp.moveaxis(o, 0, 1).reshape(B, T, KH, G, v.shape[-1])


def _mla_expand(ckv_n, w_ukv):
    B, T, _ = ckv_n.shape
    kv = (ckv_n @ w_ukv).reshape(B, T, MLA_HEADS, MLA_NOPE + MLA_V)
    return kv[..., :MLA_NOPE], kv[..., MLA_NOPE:]


def _trunk_layer(x, mod, lp, ctx):
    latent = ctx is not None
    B, T, _ = x.shape
    sh1, sc1, g1, sh2, sc2, g2 = jnp.split(mod, N_MOD, axis=-1)
    h = _rms_norm(x, lp['norm1_g']) * (1 + sc1) + sh1
    proj = h @ lp['w_in']
    offs = np.cumsum(IN_SPLITS)[:-1].tolist()
    (dn_qkv, dn_z, a_f, a_b, b_f, b_b, m_cq, m_ckv, m_kr, s_q, s_k, s_v) = jnp.split(proj, offs, axis=-1)

    qkv = _short_conv(dn_qkv, lp['dn_conv_w'])
    dq = qkv[..., :DN_HEADS * DN_DK].reshape(B, T, DN_HEADS, DN_DK)
    dk = qkv[..., DN_HEADS * DN_DK:2 * DN_HEADS * DN_DK].reshape(B, T, DN_HEADS, DN_DK)
    dv = qkv[..., 2 * DN_HEADS * DN_DK:].reshape(B, T, DN_HEADS, DN_DV)
    if latent:
        s0_f, s0_b = ctx['dn_f'], ctx['dn_b']
    else:
        s0_f = jnp.zeros((B, DN_HEADS, DN_DK, DN_DV), jnp.float32)
        s0_b = s0_f
    o_a, s_f, s_b = _deltanet(dq, dk, dv, dn_z.reshape(B, T, DN_HEADS, DN_DV), a_f, a_b, b_f, b_b,
                              lp['dn_a_log'], lp['dn_dt_bias'], lp['dn_norm_g'], s0_f, s0_b)

    q_m = (_rms_norm(m_cq, lp['mla_q_norm_g']) @ lp['mla_w_uq']).reshape(B, T, MLA_HEADS, MLA_NOPE + MLA_ROPE)
    q_nope, q_rope = q_m[..., :MLA_NOPE], q_m[..., MLA_NOPE:]
    ckv_n = _rms_norm(m_ckv, lp['mla_kv_norm_g'])
    k_nope, v_m = _mla_expand(ckv_n, lp['mla_w_ukv'])
    k_rope = m_kr[:, :, None, :]
    if latent:
        cos_m, sin_m = _axial_rope_tables(T, MLA_ROPE)
        q_rope = _apply_rope(q_rope, cos_m, sin_m)
        k_rope = _apply_rope(k_rope, cos_m, sin_m)
    q_m = jnp.concatenate([q_nope, q_rope], axis=-1)
    k_m = jnp.concatenate([k_nope, jnp.broadcast_to(k_rope, (B, T, MLA_HEADS, MLA_ROPE))], axis=-1)
    if latent:
        P = ctx['mla_ckv'].shape[1]
        kn_c, v_c = _mla_expand(ctx['mla_ckv'], lp['mla_w_ukv'])
        kr_c = jnp.broadcast_to(ctx['mla_krope'][:, :, None, :], (B, P, MLA_HEADS, MLA_ROPE))
        k_m = jnp.concatenate([jnp.concatenate([kn_c, kr_c], axis=-1), k_m], axis=1)
        v_m = jnp.concatenate([v_c, v_m], axis=1)
    o_b = _dense_attention(q_m[:, :, :, None, :], k_m, v_m, MLA_SCALE, None)

    q_c = s_q.reshape(B, T, SWA_HEADS, SWA_HD)
    k_c = s_k.reshape(B, T, SWA_KV_HEADS, SWA_HD)
    v_c = s_v.reshape(B, T, SWA_KV_HEADS, SWA_HD)
    sink = lp['swa_sink'].reshape(SWA_KV_HEADS, SWA_HEADS // SWA_KV_HEADS)
    if latent:
        cos_s, sin_s = _axial_rope_tables(T, SWA_HD)
        q_r = _apply_rope(q_c, cos_s, sin_s).reshape(B, T, SWA_KV_HEADS, SWA_HEADS // SWA_KV_HEADS, SWA_HD)
        k_r = _apply_rope(k_c, cos_s, sin_s)
        o_c = _window_attention(q_r, k_r, v_c, ctx['swa_k'], ctx['swa_v'], sink, SWA_SCALE)
    else:
        q_g = q_c.reshape(B, T, SWA_KV_HEADS, SWA_HEADS // SWA_KV_HEADS, SWA_HD)
        o_c = _dense_attention(q_g, k_c, v_c, SWA_SCALE, sink)

    mix = jnp.concatenate([o_a, o_b.reshape(B, T, -1), o_c.reshape(B, T, -1)], axis=-1) @ lp['w_out']
    x = x + g1 * mix

    h2 = _rms_norm(x, lp['norm2_g']) * (1 + sc2) + sh2
    x = x + g2 * (jnp.square(jax.nn.relu(h2 @ lp['w_ff1'])) @ lp['w_ff2'])
    if latent:
        return x, None
    return x, (jnp.stack([s_f, s_b], axis=1), ckv_n, m_kr, k_c, v_c)


def setup_inputs(seed: int = 0) -> dict:
    key = jax.random.key(seed)
    ks = jax.random.split(key, 32)
    f32 = jnp.float32
    nrm = lambda k, shape, s: jax.random.normal(k, shape, f32) * s
    gain = lambda k, shape: 1.0 + 0.05 * jax.random.normal(k, shape, f32)
    L = DEPTH
    dt = jnp.exp(jax.random.uniform(ks[14], (L, 2, DN_HEADS), f32, math.log(1e-3), math.log(1e-1)))
    return {
        'x_prompt': nrm(ks[0], (BATCH, SEQ, D_MODEL), 1.0),
        'x_sample': nrm(ks[1], (DEC_BATCH, DEC_SEQ, D_MODEL), 1.0),
        'state_dn': nrm(ks[2], (DEC_BATCH, L, 2, DN_HEADS, DN_DK, DN_DV), 0.1),
        'cache_mla_ckv': nrm(ks[3], (DEC_BATCH, L, PAST_LEN, MLA_KV_LORA), 1.0),
        'cache_mla_krope': nrm(ks[4], (DEC_BATCH, L, PAST_LEN, MLA_ROPE), 1.0),
        'cache_swa_k': nrm(ks[5], (DEC_BATCH, L, PAST_LEN, SWA_KV_HEADS, SWA_HD), 1.0),
        'cache_swa_v': nrm(ks[6], (DEC_BATCH, L, PAST_LEN, SWA_KV_HEADS, SWA_HD), 1.0),
        'c': nrm(ks[7], (DEC_BATCH, D_MODEL), 1.0),
        'c_ctx': nrm(ks[8], (D_MODEL,), 1.0),
        'ada_w': nrm(ks[9], (L, D_MODEL, N_MOD * D_MODEL), 0.5 * D_MODEL ** -0.5),
        'ada_b': nrm(ks[10], (L, N_MOD * D_MODEL), 0.02),
        'norm1_g': gain(ks[11], (L, D_MODEL)),
        'norm2_g': gain(ks[12], (L, D_MODEL)),
        'w_in': nrm(ks[13], (L, D_MODEL, IN_W), D_MODEL ** -0.5),
        'dn_conv_w': nrm(ks[15], (L, DN_CONV, DN_CONV_CH), DN_CONV ** -0.5),
        'dn_a_log': jnp.log(jax.random.uniform(ks[16], (L, 2, DN_HEADS), f32, 1.0, 16.0)),
        'dn_dt_bias': dt + jnp.log(-jnp.expm1(-dt)),
        'dn_norm_g': gain(ks[17], (L, DN_DV)),
        'mla_q_norm_g': gain(ks[18], (L, MLA_Q_LORA)),
        'mla_w_uq': nrm(ks[19], (L, MLA_Q_LORA, MLA_HEADS * (MLA_NOPE + MLA_ROPE)), MLA_Q_LORA ** -0.5),
        'mla_kv_norm_g': gain(ks[20], (L, MLA_KV_LORA)),
        'mla_w_ukv': nrm(ks[21], (L, MLA_KV_LORA, MLA_HEADS * (MLA_NOPE + MLA_V)), MLA_KV_LORA ** -0.5),
        'swa_sink': nrm(ks[22], (L, SWA_HEADS), 0.5),
        'w_out': nrm(ks[23], (L, MIX_W, D_MODEL), MIX_W ** -0.5),
        'w_ff1': nrm(ks[24], (L, D_MODEL, D_FF), D_MODEL ** -0.5),
        'w_ff2': nrm(ks[25], (L, D_FF, D_MODEL), D_FF ** -0.5),
        'final_norm_g': gain(ks[26], (D_MODEL,)),
    }


def reference(x_prompt, x_sample, state_dn, cache_mla_ckv, cache_mla_krope, cache_swa_k, cache_swa_v,
              c, c_ctx, ada_w, ada_b, norm1_g, norm2_g, w_in, dn_conv_w, dn_a_log, dn_dt_bias, dn_norm_g,
              mla_q_norm_g, mla_w_uq, mla_kv_norm_g, mla_w_ukv, swa_sink, w_out, w_ff1, w_ff2, final_norm_g):
    layers = [dict(norm1_g=norm1_g[l], norm2_g=norm2_g[l], w_in=w_in[l], dn_conv_w=dn_conv_w[l],
                   dn_a_log=dn_a_log[l], dn_dt_bias=dn_dt_bias[l], dn_norm_g=dn_norm_g[l],
                   mla_q_norm_g=mla_q_norm_g[l], mla_w_uq=mla_w_uq[l], mla_kv_norm_g=mla_kv_norm_g[l],
                   mla_w_ukv=mla_w_ukv[l], swa_sink=swa_sink[l], w_out=w_out[l],
                   w_ff1=w_ff1[l], w_ff2=w_ff2[l]) for l in range(DEPTH)]

    xp = x_prompt
    new_dn, new_ckv, new_kr, new_k, new_v = [], [], [], [], []
    for l in range(DEPTH):
        mod_ctx = jax.nn.silu(c_ctx) @ ada_w[l] + ada_b[l]
        xp, (s_dn, ckv_n, kr, k_c, v_c) = _trunk_layer(xp, mod_ctx, layers[l], None)
        new_dn.append(s_dn)
        new_ckv.append(ckv_n)
        new_kr.append(kr)
        new_k.append(k_c)
        new_v.append(v_c)
    y_prompt = _rms_norm(xp, final_norm_g)

    xs = x_sample
    for l in range(DEPTH):
        mod = (jax.nn.silu(c) @ ada_w[l] + ada_b[l])[:, None, :]
        ctx = dict(dn_f=state_dn[:, l, 0], dn_b=state_dn[:, l, 1], mla_ckv=cache_mla_ckv[:, l],
                   mla_krope=cache_mla_krope[:, l], swa_k=cache_swa_k[:, l], swa_v=cache_swa_v[:, l])
        xs, _ = _trunk_layer(xs, mod, layers[l], ctx)
    y_sample = _rms_norm(xs, final_norm_g)

    return (y_prompt, y_sample, jnp.stack(new_dn, axis=1), jnp.stack(new_ckv, axis=1),
            jnp.stack(new_kr, axis=1), jnp.stack(new_k, axis=1), jnp.stack(new_v, axis=1))
```

```python
import functools

import numpy as np
import jax
import jax.numpy as jnp
from jax import lax
from jax.experimental import pallas as pl
from jax.experimental.pallas import tpu as pltpu

F32 = jnp.float32
BF16 = jnp.bfloat16

D_MODEL = 1024
GRID_W = 64
EPS = 1e-6
ROPE_BASE = 10000.0
DN_HEADS = 4
DN_DK = 128
DN_DV = 128
DN_CHUNK = 64
MLA_HEADS = 4
MLA_Q_LORA = 256
MLA_KV_LORA = 128
MLA_NOPE = 64
MLA_ROPE = 32
MLA_V = 64
SWA_HEADS = 4
SWA_KV_HEADS = 2
SWA_HD = 64
WINDOW = 128
D_FF = 4 * D_MODEL
N_MOD = 6
MLA_SCALE = (MLA_NOPE + MLA_ROPE) ** -0.5
SWA_SCALE = SWA_HD ** -0.5

LANES = 128
MOD_ROWS = 16
NEG_BIG = -1e30
VMEM_LIMIT = 56 * 1024 * 1024

C_QKV = (0, 1536)
C_Z = (1536, 2048)
C_SMALL = (2048, 2176)
C_CQ = (2176, 2432)
C_CKV = (2432, 2560)
C_SQ = (2560, 3072)
C_SK = (3072, 3328)
C_SV = (3328, 3456)
IN_PACKED = 3456
KR_LANE = MLA_NOPE

HI = lax.Precision.HIGHEST


def _params(n_grid):
    return pltpu.CompilerParams(dimension_semantics=("arbitrary",) * n_grid,
                                vmem_limit_bytes=VMEM_LIMIT)


def _resident(shape):
    nd = len(shape)
    return pl.BlockSpec(shape, lambda *_: (0,) * nd, pipeline_mode=pl.Buffered(1))


def _dot(a, b, precision=None):
    return jnp.dot(a, b, preferred_element_type=F32, precision=precision)


def _dot_nt(a, b, precision=None):
    return lax.dot_general(a, b, (((1,), (1,)), ((), ())), preferred_element_type=F32,
                           precision=precision)


def _rms(x, g):
    return x * lax.rsqrt(jnp.mean(x * x, axis=-1, keepdims=True) + EPS) * g


def _sigmoid(x):
    return 1.0 / (1.0 + jnp.exp(-x))


def _rope(x, c, s1, s2, half):
    return x * c + pltpu.roll(x, half, 1) * s1 + pltpu.roll(x, LANES - half, 1) * s2


def _mod_kernel(c_ref, w_ref, b_ref, o_ref):
    c = c_ref[...]
    s = (c * _sigmoid(c)).astype(BF16)
    o_ref[0] = _dot(s, w_ref[0].astype(BF16)) + b_ref[0]


def _modulation(cvec, ada_w, ada_b):
    L = ada_w.shape[0]
    nb = 768
    return pl.pallas_call(
        _mod_kernel,
        grid=(L, N_MOD * D_MODEL // nb),
        in_specs=[pl.BlockSpec((MOD_ROWS, D_MODEL), lambda l, j: (0, 0)),
                  pl.BlockSpec((1, D_MODEL, nb), lambda l, j: (l, 0, j)),
                  pl.BlockSpec((1, 1, nb), lambda l, j: (l, 0, j))],
        out_specs=pl.BlockSpec((1, MOD_ROWS, nb), lambda l, j: (l, 0, j)),
        out_shape=jax.ShapeDtypeStruct((L, MOD_ROWS, N_MOD * D_MODEL), F32),
        compiler_params=_params(2),
        name="modulation",
    )(cvec, ada_w, ada_b.reshape(L, 1, N_MOD * D_MODEL))


def _v_variants(v):
    lane = lax.broadcasted_iota(jnp.int32, v.shape, 1)
    lo = lane < SWA_HD
    sw = pltpu.roll(v, SWA_HD, 1)
    zero = jnp.zeros_like(v)
    return jnp.concatenate([jnp.where(lo, v, zero), jnp.where(lo, zero, sw),
                            jnp.where(lo, sw, zero), jnp.where(lo, zero, v)], axis=1)


def _inproj_kernel(*refs, latent):
    if latent:
        (x_ref, mod_ref, g1_ref, w_ref, gq_ref, wuq_ref, gkv_ref, wuk_ref, wuv_ref,
         mc_ref, ms1_ref, ms2_ref, sc_ref, ss1_ref, ss2_ref,
         qkv_o, z_o, small_o, qm_o, km_o, vm_o, qs_o, ks_o, vs_o) = refs
    else:
        (x_ref, mod_ref, g1_ref, w_ref, gq_ref, wuq_ref, gkv_ref, wuk_ref, wuv_ref,
         qkv_o, z_o, small_o, qm_o, km_o, vm_o, qs_o, ks_o, vs_o, ckv_o, sv_o) = refs
    x = x_ref[...]
    sh1 = mod_ref[0, 0:1, :]
    sc1 = mod_ref[0, 1:2, :]
    hb = (_rms(x, g1_ref[...]) * (1.0 + sc1) + sh1).astype(BF16)

    def proj(cols):
        return _dot(hb, w_ref[:, cols[0]:cols[1]])

    qkv_o[...] = proj(C_QKV)
    z_o[...] = proj(C_Z)
    small = proj(C_SMALL)
    small_o[...] = small

    cqn = _rms(proj(C_CQ), gq_ref[...]).astype(BF16)
    q = _dot(cqn, wuq_ref[...])
    ckv_n = _rms(proj(C_CKV), gkv_ref[...])
    ckv_b = ckv_n.astype(BF16)
    kn = _dot(ckv_b, wuk_ref[...])
    vm_o[...] = _dot(ckv_b, wuv_ref[...]).astype(BF16)
    lane = lax.broadcasted_iota(jnp.int32, small.shape, 1)
    kr = jnp.where((lane >= KR_LANE) & (lane < KR_LANE + MLA_ROPE), small, 0.0)
    if latent:
        mc, ms1, ms2 = mc_ref[...], ms1_ref[...], ms2_ref[...]
        kr = _rope(kr, mc, ms1, ms2, MLA_ROPE // 2)
    for h in range(MLA_HEADS):
        sl = slice(h * LANES, (h + 1) * LANES)
        qh = q[:, sl]
        if latent:
            qh = _rope(qh, mc, ms1, ms2, MLA_ROPE // 2)
        qm_o[:, sl] = (qh * MLA_SCALE).astype(BF16)
        km_o[:, sl] = (kn[:, sl] + kr).astype(BF16)

    sq = proj(C_SQ)
    sk = proj(C_SK)
    sv = proj(C_SV)
    if latent:
        sc, ss1, ss2 = sc_ref[...], ss1_ref[...], ss2_ref[...]
    for h in range(SWA_HEADS):
        sl = slice(h * LANES, (h + 1) * LANES)
        qh = sq[:, sl]
        if latent:
            qh = _rope(qh, sc, ss1, ss2, SWA_HD // 2)
        qs_o[:, sl] = (qh * SWA_SCALE).astype(BF16)
    for j in range(SWA_KV_HEADS):
        sl = slice(j * LANES, (j + 1) * LANES)
        kh = sk[:, sl]
        if latent:
            kh = _rope(kh, sc, ss1, ss2, SWA_HD // 2)
        ks_o[:, sl] = kh.astype(ks_o.dtype)
    vs_o[...] = _v_variants(sv).astype(BF16)
    if not latent:
        ckv_o[...] = ckv_n
        sv_o[...] = sv


def _inproj(x2d, mod_l, g1, wp, tables, *, latent, seq, tm):
    n = x2d.shape[0]
    nblk = n // tm
    per_seq = seq // tm if latent else 1
    rows = lambda w: pl.BlockSpec((tm, w), lambda i: (i, 0))
    if latent:
        mod_spec = pl.BlockSpec((1, N_MOD, D_MODEL), lambda i: (i // per_seq, 0, 0))
    else:
        mod_spec = pl.BlockSpec((1, N_MOD, D_MODEL), lambda i: (MOD_ROWS // 2, 0, 0))
    in_specs = [rows(D_MODEL), mod_spec, _resident((1, D_MODEL)), _resident((D_MODEL, IN_PACKED)),
                _resident((1, MLA_Q_LORA)), _resident((MLA_Q_LORA, 4 * LANES)),
                _resident((1, MLA_KV_LORA)), _resident((MLA_KV_LORA, 4 * LANES)),
                _resident((MLA_KV_LORA, 4 * LANES))]
    args = [x2d, mod_l, g1, wp["w_in"], wp["gq"], wp["wuq"], wp["gkv"], wp["wuk"], wp["wuv"]]
    out_shape = [jax.ShapeDtypeStruct((n, 1536), F32), jax.ShapeDtypeStruct((n, 512), F32),
                 jax.ShapeDtypeStruct((n, LANES), F32),
                 jax.ShapeDtypeStruct((n, 512), BF16), jax.ShapeDtypeStruct((n, 512), BF16),
                 jax.ShapeDtypeStruct((n, 512), BF16),
                 jax.ShapeDtypeStruct((n, 512), BF16),
                 jax.ShapeDtypeStruct((n, 256), BF16 if latent else F32),
                 jax.ShapeDtypeStruct((n, 512), BF16)]
    out_specs = [rows(1536), rows(512), rows(LANES), rows(512), rows(512), rows(512),
                 rows(512), rows(256), rows(512)]
    if latent:
        tab = pl.BlockSpec((tm, LANES), lambda i: (i % per_seq, 0))
        in_specs += [tab] * 6
        args += list(tables)
    else:
        out_shape += [jax.ShapeDtypeStruct((n, LANES), F32), jax.ShapeDtypeStruct((n, LANES), F32)]
        out_specs += [rows(LANES), rows(LANES)]
    return pl.pallas_call(
        functools.partial(_inproj_kernel, latent=latent),
        grid=(nblk,), in_specs=in_specs, out_specs=out_specs, out_shape=out_shape,
        compiler_params=_params(1), name="inproj_lat" if latent else "inproj_ctx",
    )(*args)


def _cache_kernel(ckv_ref, kr_ref, sk_ref, sv_ref, wuk_ref, wuv_ref, km_o, vm_o, ks_o, vs_o):
    ckv = ckv_ref[0, 0].astype(BF16)
    kn = _dot(ckv, wuk_ref[...])
    kr = kr_ref[0, 0]
    for h in range(MLA_HEADS):
        sl = slice(h * LANES, (h + 1) * LANES)
        km_o[0, :, sl] = (kn[:, sl] + kr).astype(BF16)
    vm_o[0] = _dot(ckv, wuv_ref[...]).astype(BF16)
    sk = sk_ref[0, 0]
    lane = lax.broadcasted_iota(jnp.int32, sk.shape, 1)
    lo = lane < SWA_HD
    ks_o[0, :, 0:LANES] = jnp.where(lo, sk, 0.0).astype(BF16)
    ks_o[0, :, LANES:2 * LANES] = jnp.where(lo, pltpu.roll(sk, SWA_HD, 1), 0.0).astype(BF16)
    vs_o[0] = _v_variants(sv_ref[0, 0]).astype(BF16)


def _cache_prep(layer, ckv, kr_pad, sk, sv, wp):
    b, _, p, _ = ckv.shape
    cin = pl.BlockSpec((1, 1, p, LANES), lambda i: (i, layer, 0, 0))
    out = lambda w: pl.BlockSpec((1, p, w), lambda i: (i, 0, 0))
    return pl.pallas_call(
        _cache_kernel, grid=(b,),
        in_specs=[cin, cin, cin, cin, _resident((MLA_KV_LORA, 4 * LANES)),
                  _resident((MLA_KV_LORA, 4 * LANES))],
        out_specs=[out(512), out(512), out(256), out(512)],
        out_shape=[jax.ShapeDtypeStruct((b, p, 512), BF16), jax.ShapeDtypeStruct((b, p, 512), BF16),
                   jax.ShapeDtypeStruct((b, p, 256), BF16), jax.ShapeDtypeStruct((b, p, 512), BF16)],
        compiler_params=_params(1), name="cache_prep",
    )(ckv, kr_pad, sk, sv, wp["wuk"], wp["wuv"])


def _chunk_cumsum(x, forward):
    row = lax.broadcasted_iota(jnp.int32, x.shape, 0)
    s = 1
    while s < DN_CHUNK:
        if forward:
            x = x + jnp.where(row >= s, pltpu.roll(x, s, 0), 0.0)
        else:
            x = x + jnp.where(row < DN_CHUNK - s, pltpu.roll(x, DN_CHUNK - s, 0), 0.0)
        s *= 2
    return x


def _unit_tri_inverse(low):
    n = low.shape[0]
    eye = (lax.broadcasted_iota(jnp.int32, (n, n), 0)
           == lax.broadcasted_iota(jnp.int32, (n, n), 1)).astype(F32)
    p = -low
    t = eye + p
    s = 1
    while 2 * s < n:
        p = _dot(p, p, HI)
        t = t + _dot(t, p, HI)
        s *= 2
    return t


def _dn_chunk(S, qc, kc, vc, gcol, bcol, forward):
    C = DN_CHUNK
    ri = lax.broadcasted_iota(jnp.int32, (C, C), 0)
    ci = lax.broadcasted_iota(jnp.int32, (C, C), 1)
    incl = (ri >= ci) if forward else (ri <= ci)
    strict = (ri > ci) if forward else (ri < ci)
    gcb = _chunk_cumsum(jnp.broadcast_to(gcol, (C, LANES)), forward)
    g_row = gcb.T[:C, :]
    decay = jnp.exp(jnp.where(incl, gcb[:, :C] - g_row, NEG_BIG))
    kb = kc * bcol
    low = jnp.where(strict, _dot_nt(kb, kc) * decay, 0.0)
    tinv = _unit_tri_inverse(low)
    rhs = jnp.concatenate([vc * bcol, kb * jnp.exp(gcb)], axis=1)
    sol = _dot(tinv, rhs, HI)
    u, w = sol[:, :DN_DV], sol[:, DN_DV:]
    intra = jnp.where(incl, _dot_nt(qc, kc) * decay, 0.0)
    v_new = u - _dot(w, S)
    o = _dot(qc * jnp.exp(gcb), S) + _dot(intra, v_new)
    g_last = gcb[C - 1:C, :] if forward else gcb[0:1, :]
    k_dec = kc * jnp.exp(g_last - gcb)
    S = S * jnp.exp(g_last) + _dot(k_dec.T, v_new)
    return S, o


def _dn_kernel(*refs, seq, hb, has_s0, want_state):
    it = iter(refs)
    q_ref, k_ref, v_ref, z_ref, gate_ref = (next(it) for _ in range(5))
    cwq_ref, cwk_ref, cwv_ref = (next(it) for _ in range(3))
    alog_ref, dtb_ref, ng_ref = (next(it) for _ in range(3))
    s0_ref = next(it) if has_s0 else None
    o_ref = next(it)
    sfin_ref = next(it) if want_state else None
    qs, ks, vs, gs, of_s, ob_s, st = (next(it) for _ in range(7))

    C = DN_CHUNK
    n_chunks = seq // C
    tt = min(seq, 256)
    head0 = pl.program_id(1) * hb

    def conv_tile(src, cw_ref, i):
        r0 = pl.multiple_of(i * tt, tt)
        x = src[0, pl.ds(r0, tt), :]
        prev8 = src[0, pl.ds(pl.multiple_of(jnp.maximum(r0 - 8, 0), 8), 8), :]
        next8 = src[0, pl.ds(pl.multiple_of(jnp.minimum(r0 + tt, seq - 8), 8), 8), :]
        prev_row = jnp.where(r0 > 0, prev8[7:8, :], 0.0)
        next_row = jnp.where(r0 + tt < seq, next8[0:1, :], 0.0)
        row = lax.broadcasted_iota(jnp.int32, x.shape, 0)
        xp = jnp.where(row == 0, prev_row, pltpu.roll(x, 1, 0))
        xn = jnp.where(row == tt - 1, next_row, pltpu.roll(x, tt - 1, 0))
        y = xp * cw_ref[0:1, :] + x * cw_ref[1:2, :] + xn * cw_ref[2:3, :]
        return y * _sigmoid(y)

    def pre(i, carry):
        r0 = pl.multiple_of(i * tt, tt)
        qv = conv_tile(q_ref, cwq_ref, i)
        kv = conv_tile(k_ref, cwk_ref, i)
        vs[pl.ds(r0, tt), :] = conv_tile(v_ref, cwv_ref, i)
        for j in range(hb):
            sl = slice(j * LANES, (j + 1) * LANES)
            qh, kh = qv[:, sl], kv[:, sl]
            qs[pl.ds(r0, tt), sl] = qh * lax.rsqrt(jnp.sum(qh * qh, -1, keepdims=True) + EPS) * (DN_DK ** -0.5)
            ks[pl.ds(r0, tt), sl] = kh * lax.rsqrt(jnp.sum(kh * kh, -1, keepdims=True) + EPS)
            gt = gate_ref[0, j, pl.ds(r0, tt), :]
            lane = lax.broadcasted_iota(jnp.int32, gt.shape, 1)
            alog = jnp.where(lane == 0, alog_ref[0, head0 + j], alog_ref[1, head0 + j])
            dtb = jnp.where(lane == 0, dtb_ref[0, head0 + j], dtb_ref[1, head0 + j])
            xa = gt + dtb
            softplus = jnp.maximum(xa, 0.0) + jnp.log(1.0 + jnp.exp(-jnp.abs(xa)))
            gs[j, pl.ds(r0, tt), :] = jnp.where(lane < 2, -jnp.exp(alog) * softplus, _sigmoid(gt))
        return carry

    lax.fori_loop(0, seq // tt, pre, 0)

    for j in range(hb):
        for d in range(2):
            if has_s0:
                st[2 * j + d] = s0_ref[0, d, j]
            else:
                st[2 * j + d] = jnp.zeros((DN_DK, DN_DV), F32)

    def step(n, carry):
        for j in range(hb):
            sl = slice(j * LANES, (j + 1) * LANES)
            for d in range(2):
                c = n if d == 0 else n_chunks - 1 - n
                c0 = pl.multiple_of(c * C, C)
                g4 = gs[j, pl.ds(c0, C), :]
                S, o = _dn_chunk(st[2 * j + d], qs[pl.ds(c0, C), sl], ks[pl.ds(c0, C), sl],
                                 vs[pl.ds(c0, C), sl], g4[:, d:d + 1], g4[:, 2 + d:3 + d], d == 0)
                st[2 * j + d] = S
                dst = of_s if d == 0 else ob_s
                dst[pl.ds(c0, C), sl] = o
        return carry

    lax.fori_loop(0, n_chunks, step, 0)

    def post(i, carry):
        r0 = pl.multiple_of(i * tt, tt)
        o = of_s[pl.ds(r0, tt), :] + ob_s[pl.ds(r0, tt), :]
        z = z_ref[0, pl.ds(r0, tt), :]
        for j in range(hb):
            sl = slice(j * LANES, (j + 1) * LANES)
            o_ref[0, pl.ds(r0, tt), sl] = (_rms(o[:, sl], ng_ref[...]) * (z[:, sl] * _sigmoid(z[:, sl]))).astype(o_ref.dtype)
        return carry

    lax.fori_loop(0, seq // tt, post, 0)
    if want_state:
        for j in range(hb):
            for d in range(2):
                sfin_ref[0, d, j] = st[2 * j + d]


def _deltanet(qkv, z, gates, conv_w, a_log, dt_bias, norm_g, s0, *, hb, want_state):
    b, seq, _ = qkv.shape
    w = hb * LANES
    ng = DN_HEADS // hb
    col = lambda off: pl.BlockSpec((1, seq, w), lambda i, h: (i, 0, off * ng + h))
    cw = lambda off: pl.BlockSpec((3, w), lambda i, h: (0, off * ng + h))
    smem = pl.BlockSpec(memory_space=pltpu.SMEM)
    in_specs = [col(0), col(1), col(2), col(0),
                pl.BlockSpec((1, hb, seq, 4), lambda i, h: (i, h, 0, 0)),
                cw(0), cw(1), cw(2), smem, smem, _resident((1, DN_DV))]
    args = [qkv, qkv, qkv, z, gates, conv_w, conv_w, conv_w, a_log, dt_bias, norm_g]
    state_spec = pl.BlockSpec((1, 2, hb, DN_DK, DN_DV), lambda i, h: (i, 0, h, 0, 0))
    if s0 is not None:
        in_specs.append(state_spec)
        args.append(s0)
    out_specs = [pl.BlockSpec((1, seq, w), lambda i, h: (i, 0, h))]
    out_shape = [jax.ShapeDtypeStruct((b, seq, DN_HEADS * DN_DV), BF16)]
    if want_state:
        out_specs.append(state_spec)
        out_shape.append(jax.ShapeDtypeStruct((b, 2, DN_HEADS, DN_DK, DN_DV), F32))
    scratch = [pltpu.VMEM((seq, w), F32)] * 3 + [pltpu.VMEM((hb, seq, 4), F32)] \
        + [pltpu.VMEM((seq, w), F32)] * 2 + [pltpu.VMEM((2 * hb, DN_DK, DN_DV), F32)]
    res = pl.pallas_call(
        functools.partial(_dn_kernel, seq=seq, hb=hb, has_s0=s0 is not None, want_state=want_state),
        grid=(b, ng), in_specs=in_specs, out_specs=out_specs, out_shape=out_shape,
        scratch_shapes=scratch, compiler_params=_params(2), name="deltanet",
    )(*args)
    return res if want_state else (res[0], None)


def _attn_kernel(*refs, kv_heads, tq, seq, windowed, ctx_len, has_sink):
    it = iter(refs)
    q_ref, k_ref, v_ref = next(it), next(it), next(it)
    kc_ref, vc_ref = (next(it), next(it)) if ctx_len else (None, None)
    sink_ref = next(it) if has_sink else None
    o_ref = next(it)
    group = 4 // kv_heads
    q0 = pl.program_id(1) * tq
    if windowed:
        wl = tq + 2 * WINDOW
        ws = pl.multiple_of(jnp.clip(q0 - WINDOW, 0, seq - wl), WINDOW)
        qpos = q0 + lax.broadcasted_iota(jnp.int32, (tq, wl), 0)
        kpos = ws + lax.broadcasted_iota(jnp.int32, (tq, wl), 1)
        valid = jnp.abs(qpos - kpos) <= WINDOW
        rows = pl.ds(ws, wl)
    else:
        rows = slice(None)
    for pair in range(2):
        acc = None
        for hh in range(2):
            h = 2 * pair + hh
            kh = h // group
            qh = q_ref[0, :, h * LANES:(h + 1) * LANES]
            s = _dot_nt(qh, k_ref[0, rows, kh * LANES:(kh + 1) * LANES].astype(BF16))
            if windowed:
                s = jnp.where(valid, s, NEG_BIG)
            m = jnp.max(s, axis=-1, keepdims=True)
            if ctx_len:
                sc = _dot_nt(qh, kc_ref[0, :, kh * LANES:(kh + 1) * LANES])
                m = jnp.maximum(m, jnp.max(sc, axis=-1, keepdims=True))
            if has_sink:
                m = jnp.maximum(m, sink_ref[h])
            p = jnp.exp(s - m)
            l = jnp.sum(p, axis=-1, keepdims=True)
            pv = _dot(p.astype(BF16), v_ref[0, rows, h * LANES:(h + 1) * LANES])
            if ctx_len:
                pc = jnp.exp(sc - m)
                l = l + jnp.sum(pc, axis=-1, keepdims=True)
                pv = pv + _dot(pc.astype(BF16), vc_ref[0, :, h * LANES:(h + 1) * LANES])
            if has_sink:
                l = l + jnp.exp(sink_ref[h] - m)
            pv = pv / l
            acc = pv if acc is None else acc + pv
        o_ref[0, :, pair * LANES:(pair + 1) * LANES] = acc.astype(o_ref.dtype)


def _attention(q, k, v, kc, vc, sink, *, kv_heads, windowed, tq, name):
    b, seq, _ = q.shape
    ctx_len = 0 if kc is None else kc.shape[1]
    full = lambda a: pl.BlockSpec((1,) + a.shape[1:], lambda i, j: (i, 0, 0))
    in_specs = [pl.BlockSpec((1, tq, 4 * LANES), lambda i, j: (i, j, 0)), full(k), full(v)]
    args = [q, k, v]
    if ctx_len:
        in_specs += [full(kc), full(vc)]
        args += [kc, vc]
    if sink is not None:
        in_specs.append(pl.BlockSpec(memory_space=pltpu.SMEM))
        args.append(sink)
    return pl.pallas_call(
        functools.partial(_attn_kernel, kv_heads=kv_heads, tq=tq, seq=seq, windowed=windowed,
                          ctx_len=ctx_len, has_sink=sink is not None),
        grid=(b, seq // tq), in_specs=in_specs,
        out_specs=pl.BlockSpec((1, tq, 2 * LANES), lambda i, j: (i, j, 0)),
        out_shape=jax.ShapeDtypeStruct((b, seq, 2 * LANES), BF16),
        compiler_params=_params(2), name=name,
    )(*args)


def _ffn_kernel(*refs, final):
    if final:
        x_ref, oa_ref, ob_ref, oc_ref, mod_ref, n2_ref, wo_ref, w1_ref, w2_ref, fg_ref, o_ref = refs
    else:
        x_ref, oa_ref, ob_ref, oc_ref, mod_ref, n2_ref, wo_ref, w1_ref, w2_ref, o_ref = refs
    g1 = mod_ref[0, 2:3, :]
    sh2 = mod_ref[0, 3:4, :]
    sc2 = mod_ref[0, 4:5, :]
    g2 = mod_ref[0, 5:6, :]
    mix = (_dot(oa_ref[...], wo_ref[0:512, :]) + _dot(ob_ref[...], wo_ref[512:768, :])
           + _dot(oc_ref[...], wo_ref[768:1024, :]))
    x = x_ref[...] + g1 * mix
    h2 = (_rms(x, n2_ref[...]) * (1.0 + sc2) + sh2).astype(BF16)
    acc = None
    fc = 1024
    for c in range(D_FF // fc):
        a = jnp.maximum(_dot(h2, w1_ref[:, c * fc:(c + 1) * fc]), 0.0)
        part = _dot((a * a).astype(BF16), w2_ref[c * fc:(c + 1) * fc, :])
        acc = part if acc is None else acc + part
    x = x + g2 * acc
    if final:
        x = _rms(x, fg_ref[...])
    o_ref[...] = x


def _ffn(x2d, oa, ob, oc, mod_l, n2g, wp, final_g, *, latent, seq, tm):
    n = x2d.shape[0]
    per_seq = seq // tm if latent else 1
    rows = lambda w: pl.BlockSpec((tm, w), lambda i: (i, 0))
    if latent:
        mod_spec = pl.BlockSpec((1, N_MOD, D_MODEL), lambda i: (i // per_seq, 0, 0))
    else:
        mod_spec = pl.BlockSpec((1, N_MOD, D_MODEL), lambda i: (MOD_ROWS // 2, 0, 0))
    in_specs = [rows(D_MODEL), rows(512), rows(256), rows(256), mod_spec, _resident((1, D_MODEL)),
                _resident((D_MODEL, D_MODEL)), _resident((D_MODEL, D_FF)), _resident((D_FF, D_MODEL))]
    args = [x2d, oa, ob, oc, mod_l, n2g, wp["w_out"], wp["w_ff1"], wp["w_ff2"]]
    if final_g is not None:
        in_specs.append(_resident((1, D_MODEL)))
        args.append(final_g)
    return pl.pallas_call(
        functools.partial(_ffn_kernel, final=final_g is not None),
        grid=(n // tm,), in_specs=in_specs, out_specs=rows(D_MODEL),
        out_shape=jax.ShapeDtypeStruct((n, D_MODEL), F32),
        compiler_params=_params(1), name="ffn",
    )(*args)


def _rope_tables(n_tok, rot_dim, lane0):
    rows = n_tok // GRID_W
    row = jnp.repeat(jnp.arange(rows, dtype=F32), GRID_W)
    colp = jnp.tile(jnp.arange(GRID_W, dtype=F32), rows)
    n_freq = rot_dim // 4
    inv_freq = ROPE_BASE ** (-jnp.arange(n_freq, dtype=F32) / n_freq)
    ang = jnp.concatenate([row[:, None] * inv_freq, colp[:, None] * inv_freq], axis=-1)
    cos, sin = jnp.cos(ang), jnp.sin(ang)
    half = rot_dim // 2
    zeros = lambda w: jnp.zeros((n_tok, w), F32)
    tail = LANES - lane0 - rot_dim
    c = jnp.concatenate([jnp.ones((n_tok, lane0), F32), cos, cos, zeros(tail)], axis=1)
    s1 = jnp.concatenate([zeros(lane0 + half), sin, zeros(tail)], axis=1)
    s2 = jnp.concatenate([zeros(lane0), -sin, zeros(half + tail)], axis=1)
    return c, s1, s2


def _pad_heads(w, n_heads, width):
    k = w.shape[0]
    w = w.reshape(k, n_heads, width)
    return jnp.pad(w, ((0, 0), (0, 0), (0, LANES - width))).reshape(k, n_heads * LANES)


def _pack_layer(l, w_in, mla_q_norm_g, mla_w_uq, mla_kv_norm_g, mla_w_ukv, w_out, w_ff1, w_ff2):
    w = w_in[l]
    offs = np.cumsum([0, 1536, 512, 4, 4, 4, 4, MLA_Q_LORA, MLA_KV_LORA, MLA_ROPE, 256, 128, 128])
    piece = lambda i: w[:, offs[i]:offs[i + 1]]
    k = w.shape[0]
    small = jnp.concatenate([piece(2), piece(3), piece(4), piece(5), jnp.zeros((k, KR_LANE - 16), F32),
                             piece(8), jnp.zeros((k, LANES - KR_LANE - MLA_ROPE), F32)], axis=1)
    w_packed = jnp.concatenate([piece(0), piece(1), small, piece(6), piece(7),
                                _pad_heads(piece(9), SWA_HEADS, SWA_HD),
                                _pad_heads(piece(10), SWA_KV_HEADS, SWA_HD), piece(11)], axis=1)
    ukv = mla_w_ukv[l].reshape(MLA_KV_LORA, MLA_HEADS, MLA_NOPE + MLA_V)
    wuk = jnp.pad(ukv[:, :, :MLA_NOPE], ((0, 0), (0, 0), (0, LANES - MLA_NOPE)))
    v_part = ukv[:, :, MLA_NOPE:]
    zero = jnp.zeros_like(v_part)
    even = (jnp.arange(MLA_HEADS) % 2 == 0)[None, :, None]
    wuv = jnp.concatenate([jnp.where(even, v_part, zero), jnp.where(even, zero, v_part)], axis=2)
    return dict(
        w_in=w_packed.astype(BF16),
        gq=mla_q_norm_g[l][None, :], gkv=mla_kv_norm_g[l][None, :],
        wuq=_pad_heads(mla_w_uq[l], MLA_HEADS, MLA_NOPE + MLA_ROPE).astype(BF16),
        wuk=wuk.reshape(MLA_KV_LORA, MLA_HEADS * LANES).astype(BF16),
        wuv=wuv.reshape(MLA_KV_LORA, MLA_HEADS * LANES).astype(BF16),
        w_out=w_out[l].astype(BF16), w_ff1=w_ff1[l].astype(BF16), w_ff2=w_ff2[l].astype(BF16))


def _layer(x2d, mod_l, wp, lp, ctx, final_g, *, batch, seq, latent, tables):
    tm = 512
    outs = _inproj(x2d, mod_l, lp["norm1_g"], wp, tables, latent=latent, seq=seq, tm=tm)
    qkv, z, small, qm, km, vm, qs, ks, vs = outs[:9]
    r3 = lambda a: a.reshape(batch, seq, a.shape[-1])
    gates = small[:, :16].reshape(batch, seq, 4, DN_HEADS).transpose(0, 3, 1, 2)
    o_a, state = _deltanet(r3(qkv), r3(z), gates, lp["dn_conv_w"], lp["dn_a_log"], lp["dn_dt_bias"],
                           lp["dn_norm_g"], ctx["dn"] if latent else None,
                           hb=1 if latent else DN_HEADS, want_state=not latent)
    tq = 256
    if latent:
        o_b = _attention(r3(qm), r3(km), r3(vm), ctx["km"], ctx["vm"], None,
                         kv_heads=MLA_HEADS, windowed=False, tq=tq, name="mla_attn")
        o_c = _attention(r3(qs), r3(ks), r3(vs), ctx["ks"], ctx["vs"], lp["swa_sink"],
                         kv_heads=SWA_KV_HEADS, windowed=True, tq=tq, name="swa_attn")
    else:
        o_b = _attention(r3(qm), r3(km), r3(vm), None, None, None,
                         kv_heads=MLA_HEADS, windowed=False, tq=tq, name="mla_attn")
        o_c = _attention(r3(qs), r3(ks), r3(vs), None, None, lp["swa_sink"],
                         kv_heads=SWA_KV_HEADS, windowed=False, tq=tq, name="swa_attn")
    n = batch * seq
    x_new = _ffn(x2d, o_a.reshape(n, -1), o_b.reshape(n, -1), o_c.reshape(n, -1), mod_l,
                 lp["norm2_g"], wp, final_g, latent=latent, seq=seq, tm=tm)
    if latent:
        return x_new, None
    ckv_n, sv = outs[9], outs[10]
    k_c = ks.reshape(batch, seq, SWA_KV_HEADS, LANES)[..., :SWA_HD]
    new = (state, ckv_n.reshape(batch, seq, MLA_KV_LORA),
           small[:, KR_LANE:KR_LANE + MLA_ROPE].reshape(batch, seq, MLA_ROPE),
           k_c, sv.reshape(batch, seq, SWA_KV_HEADS, SWA_HD))
    return x_new, new


def kernel(x_prompt, x_sample, state_dn, cache_mla_ckv, cache_mla_krope, cache_swa_k, cache_swa_v,
           c, c_ctx, ada_w, ada_b, norm1_g, norm2_g, w_in, dn_conv_w, dn_a_log, dn_dt_bias, dn_norm_g,
           mla_q_norm_g, mla_w_uq, mla_kv_norm_g, mla_w_ukv, swa_sink, w_out, w_ff1, w_ff2, final_norm_g):
    depth = ada_w.shape[0]
    bp, sp, _ = x_prompt.shape
    bs, ss, _ = x_sample.shape
    p_len = cache_mla_ckv.shape[2]
    assert bs <= MOD_ROWS // 2

    cvec = jnp.zeros((MOD_ROWS, D_MODEL), F32).at[:bs].set(c).at[MOD_ROWS // 2].set(c_ctx)
    mod = _modulation(cvec, ada_w, ada_b).reshape(depth, MOD_ROWS, N_MOD, D_MODEL)

    packed = [_pack_layer(l, w_in, mla_q_norm_g, mla_w_uq, mla_kv_norm_g, mla_w_ukv, w_out, w_ff1, w_ff2)
              for l in range(depth)]
    lps = [dict(norm1_g=norm1_g[l][None, :], norm2_g=norm2_g[l][None, :], dn_conv_w=dn_conv_w[l],
                dn_a_log=dn_a_log[l], dn_dt_bias=dn_dt_bias[l], dn_norm_g=dn_norm_g[l][None, :],
                swa_sink=swa_sink[l]) for l in range(depth)]
    final_g = final_norm_g[None, :]

    xp = x_prompt.reshape(bp * sp, D_MODEL)
    news = []
    for l in range(depth):
        xp, new = _layer(xp, mod[l], packed[l], lps[l], None, final_g if l == depth - 1 else None,
                         batch=bp, seq=sp, latent=False, tables=None)
        news.append(new)
    y_prompt = xp.reshape(bp, sp, D_MODEL)

    tables = _rope_tables(ss, MLA_ROPE, KR_LANE) + _rope_tables(ss, SWA_HD, 0)
    kr_pad = jnp.pad(cache_mla_krope, ((0, 0), (0, 0), (0, 0), (KR_LANE, LANES - KR_LANE - MLA_ROPE)))
    sk_c = cache_swa_k.reshape(bs, depth, p_len, SWA_KV_HEADS * SWA_HD)
    sv_c = cache_swa_v.reshape(bs, depth, p_len, SWA_KV_HEADS * SWA_HD)
    xs = x_sample.reshape(bs * ss, D_MODEL)
    for l in range(depth):
        km, vm, ks, vs = _cache_prep(l, cache_mla_ckv, kr_pad, sk_c, sv_c, packed[l])
        ctx = dict(dn=state_dn[:, l], km=km, vm=vm, ks=ks, vs=vs)
        xs, _ = _layer(xs, mod[l], packed[l], lps[l], ctx, final_g if l == depth - 1 else None,
                       batch=bs, seq=ss, latent=True, tables=tables)
    y_sample = xs.reshape(bs, ss, D_MODEL)

    stack = lambda i: jnp.stack([nw[i] for nw in news], axis=1)
    return (y_prompt, y_sample, stack(0), stack(1), stack(2), stack(3), stack(4))
```

```python
import functools

import numpy as np
import jax
import jax.numpy as jnp
from jax import lax
from jax.experimental import pallas as pl
from jax.experimental.pallas import tpu as pltpu

F32 = jnp.float32
BF16 = jnp.bfloat16

D_MODEL = 1024
GRID_W = 64
EPS = 1e-6
ROPE_BASE = 10000.0
DN_HEADS = 4
DN_DK = 128
DN_DV = 128
DN_CHUNK = 64
MLA_HEADS = 4
MLA_Q_LORA = 256
MLA_KV_LORA = 128
MLA_NOPE = 64
MLA_ROPE = 32
MLA_V = 64
SWA_HEADS = 4
SWA_KV_HEADS = 2
SWA_HD = 64
WINDOW = 128
D_FF = 4 * D_MODEL
N_MOD = 6
MLA_SCALE = (MLA_NOPE + MLA_ROPE) ** -0.5
SWA_SCALE = SWA_HD ** -0.5

LANES = 128
MOD_ROWS = 16
NEG_BIG = -1e30
VMEM_LIMIT = 56 * 1024 * 1024

C_QKV = (0, 1536)
C_Z = (1536, 2048)
C_SMALL = (2048, 2176)
C_CQ = (2176, 2432)
C_CKV = (2432, 2560)
C_SQ = (2560, 3072)
C_SK = (3072, 3328)
C_SV = (3328, 3456)
IN_PACKED = 3456
KR_LANE = MLA_NOPE


def _params(n_grid):
    return pltpu.CompilerParams(dimension_semantics=("arbitrary",) * n_grid,
                                vmem_limit_bytes=VMEM_LIMIT)


def _resident(shape):
    nd = len(shape)
    return pl.BlockSpec(shape, lambda *_: (0,) * nd, pipeline_mode=pl.Buffered(1))


def _dot(a, b):
    return jnp.dot(a, b, preferred_element_type=F32)


def _dot_nt(a, b):
    return lax.dot_general(a, b, (((1,), (1,)), ((), ())), preferred_element_type=F32)


def _rms(x, g):
    return x * lax.rsqrt(jnp.mean(x * x, axis=-1, keepdims=True) + EPS) * g


def _sigmoid(x):
    return 1.0 / (1.0 + jnp.exp(-x))


def _rope(x, c, s1, s2, half):
    return x * c + pltpu.roll(x, half, 1) * s1 + pltpu.roll(x, LANES - half, 1) * s2


def _mod_kernel(c_ref, w_ref, b_ref, o_ref):
    c = c_ref[...]
    s = (c * _sigmoid(c)).astype(BF16)
    o_ref[0] = _dot(s, w_ref[0].astype(BF16)) + b_ref[0]


def _modulation(cvec, ada_w, ada_b):
    L = ada_w.shape[0]
    nb = 768
    return pl.pallas_call(
        _mod_kernel,
        grid=(L, N_MOD * D_MODEL // nb),
        in_specs=[pl.BlockSpec((MOD_ROWS, D_MODEL), lambda l, j: (0, 0)),
                  pl.BlockSpec((1, D_MODEL, nb), lambda l, j: (l, 0, j)),
                  pl.BlockSpec((1, 1, nb), lambda l, j: (l, 0, j))],
        out_specs=pl.BlockSpec((1, MOD_ROWS, nb), lambda l, j: (l, 0, j)),
        out_shape=jax.ShapeDtypeStruct((L, MOD_ROWS, N_MOD * D_MODEL), F32),
        compiler_params=_params(2),
        name="modulation",
    )(cvec, ada_w, ada_b.reshape(L, 1, N_MOD * D_MODEL))


def _v_variants(v):
    lane = lax.broadcasted_iota(jnp.int32, v.shape, 1)
    lo = lane < SWA_HD
    sw = pltpu.roll(v, SWA_HD, 1)
    zero = jnp.zeros_like(v)
    return jnp.concatenate([jnp.where(lo, v, zero), jnp.where(lo, zero, sw),
                            jnp.where(lo, sw, zero), jnp.where(lo, zero, v)], axis=1)


def _inproj_kernel(*refs, latent):
    if latent:
        (x_ref, mod_ref, g1_ref, w_ref, gq_ref, wuq_ref, gkv_ref, wuk_ref, wuv_ref,
         mc_ref, ms1_ref, ms2_ref, sc_ref, ss1_ref, ss2_ref,
         qkv_o, z_o, small_o, qm_o, km_o, vm_o, qs_o, ks_o, vs_o) = refs
    else:
        (x_ref, mod_ref, g1_ref, w_ref, gq_ref, wuq_ref, gkv_ref, wuk_ref, wuv_ref,
         qkv_o, z_o, small_o, qm_o, km_o, vm_o, qs_o, ks_o, vs_o, ckv_o, sv_o) = refs
    x = x_ref[...]
    sh1 = mod_ref[0, 0:1, :]
    sc1 = mod_ref[0, 1:2, :]
    hb = (_rms(x, g1_ref[...]) * (1.0 + sc1) + sh1).astype(BF16)

    def proj(cols):
        return _dot(hb, w_ref[:, cols[0]:cols[1]])

    qkv_o[...] = proj(C_QKV)
    z_o[...] = proj(C_Z)
    small = proj(C_SMALL)
    small_o[...] = small

    cqn = _rms(proj(C_CQ), gq_ref[...]).astype(BF16)
    q = _dot(cqn, wuq_ref[...])
    ckv_n = _rms(proj(C_CKV), gkv_ref[...])
    ckv_b = ckv_n.astype(BF16)
    kn = _dot(ckv_b, wuk_ref[...])
    vm_o[...] = _dot(ckv_b, wuv_ref[...]).astype(BF16)
    lane = lax.broadcasted_iota(jnp.int32, small.shape, 1)
    kr = jnp.where((lane >= KR_LANE) & (lane < KR_LANE + MLA_ROPE), small, 0.0)
    if latent:
        mc, ms1, ms2 = mc_ref[...], ms1_ref[...], ms2_ref[...]
        kr = _rope(kr, mc, ms1, ms2, MLA_ROPE // 2)
    for h in range(MLA_HEADS):
        sl = slice(h * LANES, (h + 1) * LANES)
        qh = q[:, sl]
        if latent:
            qh = _rope(qh, mc, ms1, ms2, MLA_ROPE // 2)
        qm_o[:, sl] = (qh * MLA_SCALE).astype(BF16)
        km_o[:, sl] = (kn[:, sl] + kr).astype(BF16)

    sq = proj(C_SQ)
    sk = proj(C_SK)
    sv = proj(C_SV)
    if latent:
        sc, ss1, ss2 = sc_ref[...], ss1_ref[...], ss2_ref[...]
    for h in range(SWA_HEADS):
        sl = slice(h * LANES, (h + 1) * LANES)
        qh = sq[:, sl]
        if latent:
            qh = _rope(qh, sc, ss1, ss2, SWA_HD // 2)
        qs_o[:, sl] = (qh * SWA_SCALE).astype(BF16)
    for j in range(SWA_KV_HEADS):
        sl = slice(j * LANES, (j + 1) * LANES)
        kh = sk[:, sl]
        if latent:
            kh = _rope(kh, sc, ss1, ss2, SWA_HD // 2)
        ks_o[:, sl] = kh.astype(ks_o.dtype)
    vs_o[...] = _v_variants(sv).astype(BF16)
    if not latent:
        ckv_o[...] = ckv_n
        sv_o[...] = sv


def _inproj(x2d, mod_l, g1, wp, tables, *, latent, seq, tm):
    n = x2d.shape[0]
    nblk = n // tm
    per_seq = seq // tm if latent else 1
    rows = lambda w: pl.BlockSpec((tm, w), lambda i: (i, 0))
    if latent:
        mod_spec = pl.BlockSpec((1, N_MOD, D_MODEL), lambda i: (i // per_seq, 0, 0))
    else:
        mod_spec = pl.BlockSpec((1, N_MOD, D_MODEL), lambda i: (MOD_ROWS // 2, 0, 0))
    in_specs = [rows(D_MODEL), mod_spec, _resident((1, D_MODEL)), _resident((D_MODEL, IN_PACKED)),
                _resident((1, MLA_Q_LORA)), _resident((MLA_Q_LORA, 4 * LANES)),
                _resident((1, MLA_KV_LORA)), _resident((MLA_KV_LORA, 4 * LANES)),
                _resident((MLA_KV_LORA, 4 * LANES))]
    args = [x2d, mod_l, g1, wp["w_in"], wp["gq"], wp["wuq"], wp["gkv"], wp["wuk"], wp["wuv"]]
    out_shape = [jax.ShapeDtypeStruct((n, 1536), F32), jax.ShapeDtypeStruct((n, 512), F32),
                 jax.ShapeDtypeStruct((n, LANES), F32),
                 jax.ShapeDtypeStruct((n, 512), BF16), jax.ShapeDtypeStruct((n, 512), BF16),
                 jax.ShapeDtypeStruct((n, 512), BF16),
                 jax.ShapeDtypeStruct((n, 512), BF16),
                 jax.ShapeDtypeStruct((n, 256), BF16 if latent else F32),
                 jax.ShapeDtypeStruct((n, 512), BF16)]
    out_specs = [rows(1536), rows(512), rows(LANES), rows(512), rows(512), rows(512),
                 rows(512), rows(256), rows(512)]
    if latent:
        tab = pl.BlockSpec((tm, LANES), lambda i: (i % per_seq, 0))
        in_specs += [tab] * 6
        args += list(tables)
    else:
        out_shape += [jax.ShapeDtypeStruct((n, LANES), F32), jax.ShapeDtypeStruct((n, LANES), F32)]
        out_specs += [rows(LANES), rows(LANES)]
    return pl.pallas_call(
        functools.partial(_inproj_kernel, latent=latent),
        grid=(nblk,), in_specs=in_specs, out_specs=out_specs, out_shape=out_shape,
        compiler_params=_params(1), name="inproj_lat" if latent else "inproj_ctx",
    )(*args)


def _cache_kernel(ckv_ref, kr_ref, sk_ref, sv_ref, wuk_ref, wuv_ref, km_o, vm_o, ks_o, vs_o):
    ckv = ckv_ref[0, 0].astype(BF16)
    kn = _dot(ckv, wuk_ref[...])
    kr = kr_ref[0, 0]
    for h in range(MLA_HEADS):
        sl = slice(h * LANES, (h + 1) * LANES)
        km_o[0, :, sl] = (kn[:, sl] + kr).astype(BF16)
    vm_o[0] = _dot(ckv, wuv_ref[...]).astype(BF16)
    sk = sk_ref[0, 0]
    lane = lax.broadcasted_iota(jnp.int32, sk.shape, 1)
    lo = lane < SWA_HD
    ks_o[0, :, 0:LANES] = jnp.where(lo, sk, 0.0).astype(BF16)
    ks_o[0, :, LANES:2 * LANES] = jnp.where(lo, pltpu.roll(sk, SWA_HD, 1), 0.0).astype(BF16)
    vs_o[0] = _v_variants(sv_ref[0, 0]).astype(BF16)


def _cache_prep(layer, ckv, kr_pad, sk, sv, wp):
    b, _, p, _ = ckv.shape
    cin = pl.BlockSpec((1, 1, p, LANES), lambda i: (i, layer, 0, 0))
    out = lambda w: pl.BlockSpec((1, p, w), lambda i: (i, 0, 0))
    return pl.pallas_call(
        _cache_kernel, grid=(b,),
        in_specs=[cin, cin, cin, cin, _resident((MLA_KV_LORA, 4 * LANES)),
                  _resident((MLA_KV_LORA, 4 * LANES))],
        out_specs=[out(512), out(512), out(256), out(512)],
        out_shape=[jax.ShapeDtypeStruct((b, p, 512), BF16), jax.ShapeDtypeStruct((b, p, 512), BF16),
                   jax.ShapeDtypeStruct((b, p, 256), BF16), jax.ShapeDtypeStruct((b, p, 512), BF16)],
        compiler_params=_params(1), name="cache_prep",
    )(ckv, kr_pad, sk, sv, wp["wuk"], wp["wuv"])


PAIR = 2 * DN_CHUNK


def _bmm(a, b):
    return jnp.einsum("bij,bjk->bik", a, b, preferred_element_type=F32)


def _bmm_nt(a, b):
    return jnp.einsum("bik,bjk->bij", a, b, preferred_element_type=F32)


def _split(a):
    hi = a.astype(BF16)
    return hi, (a - hi.astype(F32)).astype(BF16)


def _lhs_dup(x):
    return jnp.concatenate(_split(x), axis=2)


def _rhs_dup(x):
    hi, lo = _split(x)
    return jnp.concatenate([hi, lo, hi, lo], axis=1)


def _tri_inverse_dup(low2):
    c, n2 = low2.shape[1], low2.shape[2]
    eye2 = (lax.broadcasted_iota(jnp.int32, (c, n2), 0)
            == (lax.broadcasted_iota(jnp.int32, (c, n2), 1) & (c - 1))).astype(F32)
    pp = -low2
    tt = eye2 + pp
    pp = _bmm(_lhs_dup(pp), _rhs_dup(pp))
    s = 2
    while 2 * s < c:
        r = _bmm(_lhs_dup(pp), _rhs_dup(jnp.concatenate([pp, tt], axis=2)))
        pp = r[:, :, :n2]
        tt = tt + r[:, :, n2:]
        s *= 2
    return tt + _bmm(_lhs_dup(pp), _rhs_dup(tt))


def _chunk_cumsum(x, forward):
    rows = x.shape[0]
    rc = lax.broadcasted_iota(jnp.int32, x.shape, 0) & (DN_CHUNK - 1)
    s = 1
    while s < DN_CHUNK:
        if forward:
            x = x + jnp.where(rc >= s, pltpu.roll(x, s, 0), 0.0)
        else:
            x = x + jnp.where(rc < DN_CHUNK - s, pltpu.roll(x, rows - s, 0), 0.0)
        s *= 2
    return x


def _dn_prep_kernel(qkv_ref, prev_ref, next_ref, small_ref, cw_ref, alog_ref, dtb_ref,
                    u_o, w_o, qg_o, kdt_o, intra_o, eg_o, *, tt, tiles_per_seq):
    C = DN_CHUNK
    i = pl.program_id(0)
    first = (i % tiles_per_seq) == 0
    last = (i % tiles_per_seq) == tiles_per_seq - 1
    row = lax.broadcasted_iota(jnp.int32, (tt, LANES), 0)
    n_chunks = tt // C
    hw = DN_HEADS * LANES

    def conv(c0):
        sl = slice(c0, c0 + LANES)
        x = qkv_ref[:, sl]
        prev_row = jnp.where(first, 0.0, prev_ref[7:8, sl])
        next_row = jnp.where(last, 0.0, next_ref[0:1, sl])
        xp = jnp.where(row == 0, prev_row, pltpu.roll(x, 1, 0))
        xn = jnp.where(row == tt - 1, next_row, pltpu.roll(x, tt - 1, 0))
        y = xp * cw_ref[0:1, sl] + x * cw_ref[1:2, sl] + xn * cw_ref[2:3, sl]
        return y * _sigmoid(y)

    small = small_ref[...]
    ri = lax.broadcasted_iota(jnp.int32, (C, LANES), 0)
    cj = lax.broadcasted_iota(jnp.int32, (C, LANES), 1)
    cm = cj & (C - 1)
    incl = (ri >= cm, ri <= cm)
    strict = (ri > cm, ri < cm)
    half = (cj < C, cj >= C)

    qs, ks, vs, gcbs, betas = [], [], [], [], []
    for j in range(DN_HEADS):
        q = conv(j * LANES)
        k = conv(hw + j * LANES)
        vs.append(conv(2 * hw + j * LANES))
        qs.append(q * lax.rsqrt(jnp.sum(q * q, -1, keepdims=True) + EPS) * (DN_DK ** -0.5))
        ks.append(k * lax.rsqrt(jnp.sum(k * k, -1, keepdims=True) + EPS))
        for d in range(2):
            xa = small[:, 4 * d + j:4 * d + j + 1] + dtb_ref[d, j]
            softplus = jnp.maximum(xa, 0.0) + jnp.log(1.0 + jnp.exp(-jnp.abs(xa)))
            g = -jnp.exp(alog_ref[d, j]) * softplus
            gcbs.append(_chunk_cumsum(jnp.broadcast_to(g, (tt, LANES)), d == 0))
            betas.append(_sigmoid(small[:, 8 + 4 * d + j:8 + 4 * d + j + 1]))

    ch = lambda a, c: a[c * C:(c + 1) * C]
    hc = [(j, c) for j in range(DN_HEADS) for c in range(n_chunks)]
    kq = jnp.stack([jnp.concatenate([ch(ks[j], c), ch(qs[j], c)], axis=0).astype(BF16) for j, c in hc])
    kk = jnp.stack([jnp.concatenate([ch(ks[j], c)] * 2, axis=0).astype(BF16) for j, c in hc])
    gq = _bmm_nt(kq, kk)

    chains = [(j, c, d) for j, c in hc for d in range(2)]
    lows, rhss, decays, gcs = [], [], [], []
    for j, c, d in chains:
        b = hc.index((j, c))
        gc = ch(gcbs[2 * j + d], c)
        beta = ch(betas[2 * j + d], c)
        g_keys = jnp.concatenate([gc, gc], axis=0).T[:C]
        decay = jnp.exp(jnp.where(incl[d], gc - g_keys, NEG_BIG))
        lows.append(jnp.where(strict[d], gq[b, :C] * beta * decay, 0.0))
        kc = ch(ks[j], c)
        rhss.append(jnp.concatenate([ch(vs[j], c) * beta, kc * beta * jnp.exp(gc)], axis=1))
        decays.append(decay)
        gcs.append(gc)
    tinv = _tri_inverse_dup(jnp.stack(lows))
    sol = _bmm(_lhs_dup(tinv), _rhs_dup(jnp.stack(rhss)))

    kdec = {}
    for n, (j, c, d) in enumerate(chains):
        b = hc.index((j, c))
        rows = slice(c * C, (c + 1) * C)
        cols = slice(j * LANES, (j + 1) * LANES)
        gc = gcs[n]
        u_o[d, rows, cols] = sol[n, :, :DN_DV]
        w_o[d, rows, cols] = sol[n, :, DN_DV:].astype(BF16)
        intra_o[d, rows, cols] = jnp.where(incl[d] & half[c % 2], gq[b, C:] * decays[n], 0.0).astype(BF16)
        qg_o[d, rows, cols] = (ch(qs[j], c) * jnp.exp(gc)).astype(BF16)
        g_last = gc[C - 1:C, :] if d == 0 else gc[0:1, :]
        kdec[(j, c, d)] = ch(ks[j], c) * jnp.exp(g_last - gc)
        eg_o[d, c * 8:(c + 1) * 8, cols] = jnp.broadcast_to(jnp.exp(g_last), (8, LANES))
    for j in range(DN_HEADS):
        for m in range(n_chunks // 2):
            for d in range(2):
                slab = jnp.concatenate([kdec[(j, 2 * m, d)], kdec[(j, 2 * m + 1, d)]], axis=0)
                kdt_o[d, m * PAIR:(m + 1) * PAIR, j * LANES:(j + 1) * LANES] = slab.T.astype(BF16)


def _dn_prep(qkv, small, conv_w, a_log, dt_bias, *, seq):
    n = qkv.shape[0]
    tt = 256
    t8 = tt // 8
    last8 = n // 8 - 1
    hw = DN_HEADS * LANES
    smem = pl.BlockSpec(memory_space=pltpu.SMEM)
    out = lambda rows: pl.BlockSpec((2, rows, hw), lambda i: (0, i, 0))
    big = lambda dt: jax.ShapeDtypeStruct((2, n, hw), dt)
    return pl.pallas_call(
        functools.partial(_dn_prep_kernel, tt=tt, tiles_per_seq=seq // tt),
        grid=(n // tt,),
        in_specs=[pl.BlockSpec((tt, 3 * hw), lambda i: (i, 0)),
                  pl.BlockSpec((8, 3 * hw), lambda i: (jnp.maximum(i * t8 - 1, 0), 0)),
                  pl.BlockSpec((8, 3 * hw), lambda i: (jnp.minimum((i + 1) * t8, last8), 0)),
                  pl.BlockSpec((tt, LANES), lambda i: (i, 0)),
                  _resident((3, 3 * hw)), smem, smem],
        out_specs=[out(tt)] * 5 + [out(t8)],
        out_shape=[big(F32), big(BF16), big(BF16), big(BF16), big(BF16),
                   jax.ShapeDtypeStruct((2, n // 8, hw), F32)],
        compiler_params=_params(1), name="dn_prep",
    )(qkv, qkv, qkv, small, conv_w, a_log, dt_bias)


def _dn_scan_kernel(*refs, ts, has_s0, want_state):
    it = iter(refs)
    ops = [[next(it) for _ in range(6)] for _ in range(2)]
    s0_ref = next(it) if has_s0 else None
    o_refs = [next(it), next(it)]
    sfin_ref = next(it) if want_state else None
    st = next(it)
    i = pl.program_id(1)
    chains = [(d, h) for d in range(2) for h in range(DN_HEADS)]

    @pl.when(i == 0)
    def _():
        for n, (d, h) in enumerate(chains):
            st[n] = s0_ref[0, d, h] if has_s0 else jnp.zeros((DN_DK, DN_DV), F32)

    n_chunks = ts // DN_CHUNK
    zeros = jnp.zeros((DN_HEADS, DN_CHUNK, DN_DV), F32)
    S = st[...]
    for step in range(n_chunks):
        wq, us, intras, kdts, egs = [], [], [], [], []
        for d, h in chains:
            u_ref, w_ref, qg_ref, kdt_ref, intra_ref, eg_ref = ops[d]
            c = step if d == 0 else n_chunks - 1 - step
            rows = slice(c * DN_CHUNK, (c + 1) * DN_CHUNK)
            prow = slice((c // 2) * PAIR, (c // 2 + 1) * PAIR)
            cols = slice(h * LANES, (h + 1) * LANES)
            wq.append(jnp.concatenate([w_ref[0, rows, cols], qg_ref[0, rows, cols]], axis=0))
            us.append(u_ref[0, rows, cols])
            intras.append(intra_ref[0, rows, cols])
            kdts.append(kdt_ref[0, prow, cols])
            egs.append(eg_ref[0, c * 8:c * 8 + 1, cols])
        r = _bmm(jnp.stack(wq), S.astype(BF16))
        v_new = jnp.stack(us) - r[:, :DN_CHUNK]
        fwd, bwd = v_new[:DN_HEADS], v_new[DN_HEADS:]
        c_f, c_b = step, n_chunks - 1 - step
        ext = lambda v, c: jnp.concatenate([v, zeros] if c % 2 == 0 else [zeros, v], axis=1)
        v_ext = jnp.concatenate([ext(fwd, c_f), ext(bwd, c_b)], axis=0).astype(BF16)
        o = r[:, DN_CHUNK:] + _bmm(jnp.stack(intras), v_ext)
        S = S * jnp.stack(egs) + _bmm(jnp.stack(kdts), v_ext)
        for n, (d, h) in enumerate(chains):
            c = step if d == 0 else n_chunks - 1 - step
            o_refs[d][c * DN_CHUNK:(c + 1) * DN_CHUNK, h * LANES:(h + 1) * LANES] = o[n]
    st[...] = S

    if want_state:
        @pl.when(i == pl.num_programs(1) - 1)
        def _():
            for n, (d, h) in enumerate(chains):
                sfin_ref[0, d, h] = S[n]


def _dn_scan(prep, s0, *, batch, seq, want_state):
    u, w, qg, kdt, intra, eg = prep
    ts = min(seq, 512)
    nt = seq // ts
    hw = DN_HEADS * LANES
    in_specs, args = [], []
    for d in range(2):
        tile = (lambda b, i: (0, b * nt + i, 0)) if d == 0 else (lambda b, i: (1, b * nt + nt - 1 - i, 0))
        for a in (u, w, qg, kdt, intra):
            in_specs.append(pl.BlockSpec((1, ts, hw), tile))
            args.append(a)
        in_specs.append(pl.BlockSpec((1, ts // 8, hw), tile))
        args.append(eg)
    state_spec = pl.BlockSpec((1, 2, DN_HEADS, DN_DK, DN_DV), lambda b, i: (b, 0, 0, 0, 0))
    if s0 is not None:
        in_specs.append(state_spec)
        args.append(s0)
    out_specs = [pl.BlockSpec((ts, hw), lambda b, i: (b * nt + i, 0)),
                 pl.BlockSpec((ts, hw), lambda b, i: (b * nt + nt - 1 - i, 0))]
    out_shape = [jax.ShapeDtypeStruct((batch * seq, hw), F32)] * 2
    if want_state:
        out_specs.append(state_spec)
        out_shape.append(jax.ShapeDtypeStruct((batch, 2, DN_HEADS, DN_DK, DN_DV), F32))
    res = pl.pallas_call(
        functools.partial(_dn_scan_kernel, ts=ts, has_s0=s0 is not None, want_state=want_state),
        grid=(batch, nt), in_specs=in_specs, out_specs=out_specs, out_shape=out_shape,
        scratch_shapes=[pltpu.VMEM((2 * DN_HEADS, DN_DK, DN_DV), F32)],
        compiler_params=_params(2), name="dn_scan",
    )(*args)
    return (res[0], res[1], res[2] if want_state else None)


def _attn_kernel(*refs, kv_heads, tq, seq, windowed, ctx_len, has_sink):
    it = iter(refs)
    q_ref, k_ref, v_ref = next(it), next(it), next(it)
    kc_ref, vc_ref = (next(it), next(it)) if ctx_len else (None, None)
    sink_ref = next(it) if has_sink else None
    o_ref = next(it)
    group = 4 // kv_heads
    q0 = pl.program_id(1) * tq
    if windowed:
        wl = tq + 2 * WINDOW
        ws = pl.multiple_of(jnp.clip(q0 - WINDOW, 0, seq - wl), WINDOW)
        qpos = q0 + lax.broadcasted_iota(jnp.int32, (tq, wl), 0)
        kpos = ws + lax.broadcasted_iota(jnp.int32, (tq, wl), 1)
        valid = jnp.abs(qpos - kpos) <= WINDOW
        rows = pl.ds(ws, wl)
    else:
        rows = slice(None)
    for pair in range(2):
        acc = None
        for hh in range(2):
            h = 2 * pair + hh
            kh = h // group
            qh = q_ref[0, :, h * LANES:(h + 1) * LANES]
            s = _dot_nt(qh, k_ref[0, rows, kh * LANES:(kh + 1) * LANES].astype(BF16))
            if windowed:
                s = jnp.where(valid, s, NEG_BIG)
            m = jnp.max(s, axis=-1, keepdims=True)
            if ctx_len:
                sc = _dot_nt(qh, kc_ref[0, :, kh * LANES:(kh + 1) * LANES])
                m = jnp.maximum(m, jnp.max(sc, axis=-1, keepdims=True))
            if has_sink:
                m = jnp.maximum(m, sink_ref[h])
            p = jnp.exp(s - m)
            l = jnp.sum(p, axis=-1, keepdims=True)
            pv = _dot(p.astype(BF16), v_ref[0, rows, h * LANES:(h + 1) * LANES])
            if ctx_len:
                pc = jnp.exp(sc - m)
                l = l + jnp.sum(pc, axis=-1, keepdims=True)
                pv = pv + _dot(pc.astype(BF16), vc_ref[0, :, h * LANES:(h + 1) * LANES])
            if has_sink:
                l = l + jnp.exp(sink_ref[h] - m)
            pv = pv / l
            acc = pv if acc is None else acc + pv
        o_ref[0, :, pair * LANES:(pair + 1) * LANES] = acc.astype(o_ref.dtype)


def _attention(q, k, v, kc, vc, sink, *, kv_heads, windowed, tq, name):
    b, seq, _ = q.shape
    ctx_len = 0 if kc is None else kc.shape[1]
    full = lambda a: pl.BlockSpec((1,) + a.shape[1:], lambda i, j: (i, 0, 0))
    in_specs = [pl.BlockSpec((1, tq, 4 * LANES), lambda i, j: (i, j, 0)), full(k), full(v)]
    args = [q, k, v]
    if ctx_len:
        in_specs += [full(kc), full(vc)]
        args += [kc, vc]
    if sink is not None:
        in_specs.append(pl.BlockSpec(memory_space=pltpu.SMEM))
        args.append(sink)
    return pl.pallas_call(
        functools.partial(_attn_kernel, kv_heads=kv_heads, tq=tq, seq=seq, windowed=windowed,
                          ctx_len=ctx_len, has_sink=sink is not None),
        grid=(b, seq // tq), in_specs=in_specs,
        out_specs=pl.BlockSpec((1, tq, 2 * LANES), lambda i, j: (i, j, 0)),
        out_shape=jax.ShapeDtypeStruct((b, seq, 2 * LANES), BF16),
        compiler_params=_params(2), name=name,
    )(*args)


def _ffn_kernel(*refs, final):
    if final:
        (x_ref, of_ref, obw_ref, z_ref, ng_ref, ob_ref, oc_ref, mod_ref, n2_ref, wo_ref, w1_ref, w2_ref,
         fg_ref, o_ref) = refs
    else:
        (x_ref, of_ref, obw_ref, z_ref, ng_ref, ob_ref, oc_ref, mod_ref, n2_ref, wo_ref, w1_ref, w2_ref,
         o_ref) = refs
    g1 = mod_ref[0, 2:3, :]
    sh2 = mod_ref[0, 3:4, :]
    sc2 = mod_ref[0, 4:5, :]
    g2 = mod_ref[0, 5:6, :]
    mix = _dot(ob_ref[...], wo_ref[512:768, :]) + _dot(oc_ref[...], wo_ref[768:1024, :])
    for h in range(DN_HEADS):
        sl = slice(h * LANES, (h + 1) * LANES)
        z = z_ref[:, sl]
        oa = _rms(of_ref[:, sl] + obw_ref[:, sl], ng_ref[...]) * (z * _sigmoid(z))
        mix = mix + _dot(oa.astype(BF16), wo_ref[sl, :])
    x = x_ref[...] + g1 * mix
    h2 = (_rms(x, n2_ref[...]) * (1.0 + sc2) + sh2).astype(BF16)
    acc = None
    fc = 1024
    for c in range(D_FF // fc):
        a = jnp.maximum(_dot(h2, w1_ref[:, c * fc:(c + 1) * fc]), 0.0)
        part = _dot((a * a).astype(BF16), w2_ref[c * fc:(c + 1) * fc, :])
        acc = part if acc is None else acc + part
    x = x + g2 * acc
    if final:
        x = _rms(x, fg_ref[...])
    o_ref[...] = x


def _ffn(x2d, o_fwd, o_bwd, z, dn_g, ob, oc, mod_l, n2g, wp, final_g, *, latent, seq, tm):
    n = x2d.shape[0]
    per_seq = seq // tm if latent else 1
    rows = lambda w: pl.BlockSpec((tm, w), lambda i: (i, 0))
    if latent:
        mod_spec = pl.BlockSpec((1, N_MOD, D_MODEL), lambda i: (i // per_seq, 0, 0))
    else:
        mod_spec = pl.BlockSpec((1, N_MOD, D_MODEL), lambda i: (MOD_ROWS // 2, 0, 0))
    in_specs = [rows(D_MODEL), rows(512), rows(512), rows(512), _resident((1, DN_DV)), rows(256), rows(256),
                mod_spec, _resident((1, D_MODEL)),
                _resident((D_MODEL, D_MODEL)), _resident((D_MODEL, D_FF)), _resident((D_FF, D_MODEL))]
    args = [x2d, o_fwd, o_bwd, z, dn_g, ob, oc, mod_l, n2g, wp["w_out"], wp["w_ff1"], wp["w_ff2"]]
    if final_g is not None:
        in_specs.append(_resident((1, D_MODEL)))
        args.append(final_g)
    return pl.pallas_call(
        functools.partial(_ffn_kernel, final=final_g is not None),
        grid=(n // tm,), in_specs=in_specs, out_specs=rows(D_MODEL),
        out_shape=jax.ShapeDtypeStruct((n, D_MODEL), F32),
        compiler_params=_params(1), name="ffn",
    )(*args)


def _rope_tables(n_tok, rot_dim, lane0):
    rows = n_tok // GRID_W
    row = jnp.repeat(jnp.arange(rows, dtype=F32), GRID_W)
    colp = jnp.tile(jnp.arange(GRID_W, dtype=F32), rows)
    n_freq = rot_dim // 4
    inv_freq = ROPE_BASE ** (-jnp.arange(n_freq, dtype=F32) / n_freq)
    ang = jnp.concatenate([row[:, None] * inv_freq, colp[:, None] * inv_freq], axis=-1)
    cos, sin = jnp.cos(ang), jnp.sin(ang)
    half = rot_dim // 2
    zeros = lambda w: jnp.zeros((n_tok, w), F32)
    tail = LANES - lane0 - rot_dim
    c = jnp.concatenate([jnp.ones((n_tok, lane0), F32), cos, cos, zeros(tail)], axis=1)
    s1 = jnp.concatenate([zeros(lane0 + half), sin, zeros(tail)], axis=1)
    s2 = jnp.concatenate([zeros(lane0), -sin, zeros(half + tail)], axis=1)
    return c, s1, s2


def _pad_heads(w, n_heads, width):
    k = w.shape[0]
    w = w.reshape(k, n_heads, width)
    return jnp.pad(w, ((0, 0), (0, 0), (0, LANES - width))).reshape(k, n_heads * LANES)


def _pack_layer(l, w_in, mla_q_norm_g, mla_w_uq, mla_kv_norm_g, mla_w_ukv, w_out, w_ff1, w_ff2):
    w = w_in[l]
    offs = np.cumsum([0, 1536, 512, 4, 4, 4, 4, MLA_Q_LORA, MLA_KV_LORA, MLA_ROPE, 256, 128, 128])
    piece = lambda i: w[:, offs[i]:offs[i + 1]]
    k = w.shape[0]
    small = jnp.concatenate([piece(2), piece(3), piece(4), piece(5), jnp.zeros((k, KR_LANE - 16), F32),
                             piece(8), jnp.zeros((k, LANES - KR_LANE - MLA_ROPE), F32)], axis=1)
    w_packed = jnp.concatenate([piece(0), piece(1), small, piece(6), piece(7),
                                _pad_heads(piece(9), SWA_HEADS, SWA_HD),
                                _pad_heads(piece(10), SWA_KV_HEADS, SWA_HD), piece(11)], axis=1)
    ukv = mla_w_ukv[l].reshape(MLA_KV_LORA, MLA_HEADS, MLA_NOPE + MLA_V)
    wuk = jnp.pad(ukv[:, :, :MLA_NOPE], ((0, 0), (0, 0), (0, LANES - MLA_NOPE)))
    v_part = ukv[:, :, MLA_NOPE:]
    zero = jnp.zeros_like(v_part)
    even = (jnp.arange(MLA_HEADS) % 2 == 0)[None, :, None]
    wuv = jnp.concatenate([jnp.where(even, v_part, zero), jnp.where(even, zero, v_part)], axis=2)
    return dict(
        w_in=w_packed.astype(BF16),
        gq=mla_q_norm_g[l][None, :], gkv=mla_kv_norm_g[l][None, :],
        wuq=_pad_heads(mla_w_uq[l], MLA_HEADS, MLA_NOPE + MLA_ROPE).astype(BF16),
        wuk=wuk.reshape(MLA_KV_LORA, MLA_HEADS * LANES).astype(BF16),
        wuv=wuv.reshape(MLA_KV_LORA, MLA_HEADS * LANES).astype(BF16),
        w_out=w_out[l].astype(BF16), w_ff1=w_ff1[l].astype(BF16), w_ff2=w_ff2[l].astype(BF16))


def _layer(x2d, mod_l, wp, lp, ctx, final_g, *, batch, seq, latent, tables):
    tm = 512
    outs = _inproj(x2d, mod_l, lp["norm1_g"], wp, tables, latent=latent, seq=seq, tm=tm)
    qkv, z, small, qm, km, vm, qs, ks, vs = outs[:9]
    r3 = lambda a: a.reshape(batch, seq, a.shape[-1])
    prep = _dn_prep(qkv, small, lp["dn_conv_w"], lp["dn_a_log"], lp["dn_dt_bias"], seq=seq)
    o_fwd, o_bwd, state = _dn_scan(prep, ctx["dn"] if latent else None, batch=batch, seq=seq,
                                   want_state=not latent)
    tq = 256
    if latent:
        o_b = _attention(r3(qm), r3(km), r3(vm), ctx["km"], ctx["vm"], None,
                         kv_heads=MLA_HEADS, windowed=False, tq=tq, name="mla_attn")
        o_c = _attention(r3(qs), r3(ks), r3(vs), ctx["ks"], ctx["vs"], lp["swa_sink"],
                         kv_heads=SWA_KV_HEADS, windowed=True, tq=tq, name="swa_attn")
    else:
        o_b = _attention(r3(qm), r3(km), r3(vm), None, None, None,
                         kv_heads=MLA_HEADS, windowed=False, tq=tq, name="mla_attn")
        o_c = _attention(r3(qs), r3(ks), r3(vs), None, None, lp["swa_sink"],
                         kv_heads=SWA_KV_HEADS, windowed=False, tq=tq, name="swa_attn")
    n = batch * seq
    x_new = _ffn(x2d, o_fwd, o_bwd, z, lp["dn_norm_g"], o_b.reshape(n, -1), o_c.reshape(n, -1), mod_l,
                 lp["norm2_g"], wp, final_g, latent=latent, seq=seq, tm=tm)
    if latent:
        return x_new, None
    ckv_n, sv = outs[9], outs[10]
    k_c = ks.reshape(batch, seq, SWA_KV_HEADS, LANES)[..., :SWA_HD]
    new = (state, ckv_n.reshape(batch, seq, MLA_KV_LORA),
           small[:, KR_LANE:KR_LANE + MLA_ROPE].reshape(batch, seq, MLA_ROPE),
           k_c, sv.reshape(batch, seq, SWA_KV_HEADS, SWA_HD))
    return x_new, new


def kernel(x_prompt, x_sample, state_dn, cache_mla_ckv, cache_mla_krope, cache_swa_k, cache_swa_v,
           c, c_ctx, ada_w, ada_b, norm1_g, norm2_g, w_in, dn_conv_w, dn_a_log, dn_dt_bias, dn_norm_g,
           mla_q_norm_g, mla_w_uq, mla_kv_norm_g, mla_w_ukv, swa_sink, w_out, w_ff1, w_ff2, final_norm_g):
    depth = ada_w.shape[0]
    bp, sp, _ = x_prompt.shape
    bs, ss, _ = x_sample.shape
    p_len = cache_mla_ckv.shape[2]
    assert bs <= MOD_ROWS // 2

    cvec = jnp.zeros((MOD_ROWS, D_MODEL), F32).at[:bs].set(c).at[MOD_ROWS // 2].set(c_ctx)
    mod = _modulation(cvec, ada_w, ada_b).reshape(depth, MOD_ROWS, N_MOD, D_MODEL)

    packed = [_pack_layer(l, w_in, mla_q_norm_g, mla_w_uq, mla_kv_norm_g, mla_w_ukv, w_out, w_ff1, w_ff2)
              for l in range(depth)]
    lps = [dict(norm1_g=norm1_g[l][None, :], norm2_g=norm2_g[l][None, :], dn_conv_w=dn_conv_w[l],
                dn_a_log=dn_a_log[l], dn_dt_bias=dn_dt_bias[l], dn_norm_g=dn_norm_g[l][None, :],
                swa_sink=swa_sink[l]) for l in range(depth)]
    final_g = final_norm_g[None, :]

    xp = x_prompt.reshape(bp * sp, D_MODEL)
    news = []
    for l in range(depth):
        xp, new = _layer(xp, mod[l], packed[l], lps[l], None, final_g if l == depth - 1 else None,
                         batch=bp, seq=sp, latent=False, tables=None)
        news.append(new)
    y_prompt = xp.reshape(bp, sp, D_MODEL)

    tables = _rope_tables(ss, MLA_ROPE, KR_LANE) + _rope_tables(ss, SWA_HD, 0)
    kr_pad = jnp.pad(cache_mla_krope, ((0, 0), (0, 0), (0, 0), (KR_LANE, LANES - KR_LANE - MLA_ROPE)))
    sk_c = cache_swa_k.reshape(bs, depth, p_len, SWA_KV_HEADS * SWA_HD)
    sv_c = cache_swa_v.reshape(bs, depth, p_len, SWA_KV_HEADS * SWA_HD)
    xs = x_sample.reshape(bs * ss, D_MODEL)
    for l in range(depth):
        km, vm, ks, vs = _cache_prep(l, cache_mla_ckv, kr_pad, sk_c, sv_c, packed[l])
        ctx = dict(dn=state_dn[:, l], km=km, vm=vm, ks=ks, vs=vs)
        xs, _ = _layer(xs, mod[l], packed[l], lps[l], ctx, final_g if l == depth - 1 else None,
                       batch=bs, seq=ss, latent=True, tables=tables)
    y_sample = xs.reshape(bs, ss, D_MODEL)

    stack = lambda i: jnp.stack([nw[i] for nw in news], axis=1)
    return (y_prompt, y_sample, stack(0), stack(1), stack(2), stack(3), stack(4))
```

```python
import functools

import numpy as np
import jax
import jax.numpy as jnp
from jax import lax
from jax.experimental import pallas as pl
from jax.experimental.pallas import tpu as pltpu

F32 = jnp.float32
BF16 = jnp.bfloat16

D_MODEL = 1024
GRID_W = 64
EPS = 1e-6
ROPE_BASE = 10000.0
DN_HEADS = 4
DN_DK = 128
DN_DV = 128
DN_CHUNK = 64
MLA_HEADS = 4
MLA_Q_LORA = 256
MLA_KV_LORA = 128
MLA_NOPE = 64
MLA_ROPE = 32
MLA_V = 64
SWA_HEADS = 4
SWA_KV_HEADS = 2
SWA_HD = 64
WINDOW = 128
D_FF = 4 * D_MODEL
N_MOD = 6
LOG2E = 1.4426950408889634
MLA_SCALE = (MLA_NOPE + MLA_ROPE) ** -0.5
SWA_SCALE = SWA_HD ** -0.5

LANES = 128
MOD_ROWS = 16
NEG_BIG = -1e30
VMEM_LIMIT = 56 * 1024 * 1024

C_QKV = (0, 1536)
C_Z = (1536, 2048)
C_SMALL = (2048, 2176)
C_CQ = (2176, 2432)
C_CKV = (2432, 2560)
C_SQ = (2560, 3072)
C_SK = (3072, 3328)
C_SV = (3328, 3456)
IN_PACKED = 3456
KR_LANE = MLA_NOPE
VT_ROWS = MLA_V + 16


def _params(n_grid):
    return pltpu.CompilerParams(dimension_semantics=("arbitrary",) * n_grid,
                                vmem_limit_bytes=VMEM_LIMIT)


def _resident(shape):
    nd = len(shape)
    return pl.BlockSpec(shape, lambda *_: (0,) * nd, pipeline_mode=pl.Buffered(1))


def _dot(a, b):
    return jnp.dot(a, b, preferred_element_type=F32)


def _dot_nt(a, b):
    return lax.dot_general(a, b, (((1,), (1,)), ((), ())), preferred_element_type=F32)


def _rms(x, g):
    return x * lax.rsqrt(jnp.mean(x * x, axis=-1, keepdims=True) + EPS) * g


def _sigmoid(x):
    return 1.0 / (1.0 + jnp.exp(-x))


def _rope(x, c, s1, s2, half):
    return x * c + pltpu.roll(x, half, 1) * s1 + pltpu.roll(x, LANES - half, 1) * s2


def _mod_kernel(c_ref, w_ref, b_ref, o_ref):
    c = c_ref[...]
    s = (c * _sigmoid(c)).astype(BF16)
    o_ref[0] = _dot(s, w_ref[0].astype(BF16)) + b_ref[0]


def _modulation(cvec, ada_w, ada_b):
    L = ada_w.shape[0]
    nb = 768
    return pl.pallas_call(
        _mod_kernel,
        grid=(L, N_MOD * D_MODEL // nb),
        in_specs=[pl.BlockSpec((MOD_ROWS, D_MODEL), lambda l, j: (0, 0)),
                  pl.BlockSpec((1, D_MODEL, nb), lambda l, j: (l, 0, j)),
                  pl.BlockSpec((1, 1, nb), lambda l, j: (l, 0, j))],
        out_specs=pl.BlockSpec((1, MOD_ROWS, nb), lambda l, j: (l, 0, j)),
        out_shape=jax.ShapeDtypeStruct((L, MOD_ROWS, N_MOD * D_MODEL), F32),
        compiler_params=_params(2),
        name="modulation",
    )(cvec, ada_w, ada_b.reshape(L, 1, N_MOD * D_MODEL))


def _v_variants(v):
    lane = lax.broadcasted_iota(jnp.int32, v.shape, 1)
    lo = lane < SWA_HD
    sw = pltpu.roll(v, SWA_HD, 1)
    zero = jnp.zeros_like(v)
    return jnp.concatenate([jnp.where(lo, v, zero), jnp.where(lo, zero, sw),
                            jnp.where(lo, sw, zero), jnp.where(lo, zero, v)], axis=1)


def _store_vt(vt_ref, wuv_t, ckv):
    vt = _dot_nt(wuv_t, ckv)
    ones = jnp.ones((VT_ROWS - MLA_V, vt.shape[1]), BF16)
    for h in range(MLA_HEADS):
        vt_ref[h * VT_ROWS:h * VT_ROWS + MLA_V, :] = vt[h * MLA_V:(h + 1) * MLA_V].astype(BF16)
        vt_ref[h * VT_ROWS + MLA_V:(h + 1) * VT_ROWS, :] = ones


def _inproj_kernel(*refs, latent):
    if latent:
        (x_ref, mod_ref, g1_ref, w_ref, gq_ref, wuq_ref, gkv_ref, wuk_ref, wuv_ref,
         mc_ref, ms1_ref, ms2_ref, sc_ref, ss1_ref, ss2_ref,
         qkv_o, z_o, small_o, qm_o, km_o, vm_o, qs_o, ks_o, vs_o) = refs
    else:
        (x_ref, mod_ref, g1_ref, w_ref, gq_ref, wuq_ref, gkv_ref, wuk_ref, wuv_ref,
         qkv_o, z_o, small_o, qm_o, km_o, vm_o, qs_o, ks_o, vs_o, ckv_o, sv_o) = refs
    x = x_ref[...]
    sh1 = mod_ref[0, 0:1, :]
    sc1 = mod_ref[0, 1:2, :]
    hb = (_rms(x, g1_ref[...]) * (1.0 + sc1) + sh1).astype(BF16)

    def proj(cols):
        return _dot(hb, w_ref[:, cols[0]:cols[1]])

    qkv_o[...] = proj(C_QKV)
    z_o[...] = proj(C_Z)
    small = proj(C_SMALL)
    small_o[...] = small

    cqn = _rms(proj(C_CQ), gq_ref[...]).astype(BF16)
    q = _dot(cqn, wuq_ref[...])
    ckv_n = _rms(proj(C_CKV), gkv_ref[...])
    ckv_b = ckv_n.astype(BF16)
    kn = _dot(ckv_b, wuk_ref[...])
    _store_vt(vm_o, wuv_ref[...], ckv_b)
    lane = lax.broadcasted_iota(jnp.int32, small.shape, 1)
    kr = jnp.where((lane >= KR_LANE) & (lane < KR_LANE + MLA_ROPE), small, 0.0)
    if latent:
        mc, ms1, ms2 = mc_ref[...], ms1_ref[...], ms2_ref[...]
        kr = _rope(kr, mc, ms1, ms2, MLA_ROPE // 2)
    for h in range(MLA_HEADS):
        sl = slice(h * LANES, (h + 1) * LANES)
        qh = q[:, sl]
        if latent:
            qh = _rope(qh, mc, ms1, ms2, MLA_ROPE // 2)
        qm_o[:, sl] = (qh * (MLA_SCALE * LOG2E)).astype(BF16)
        km_o[:, sl] = (kn[:, sl] + kr).astype(BF16)

    sq = proj(C_SQ)
    sk = proj(C_SK)
    sv = proj(C_SV)
    if latent:
        sc, ss1, ss2 = sc_ref[...], ss1_ref[...], ss2_ref[...]
    for h in range(SWA_HEADS):
        sl = slice(h * LANES, (h + 1) * LANES)
        qh = sq[:, sl]
        if latent:
            qh = _rope(qh, sc, ss1, ss2, SWA_HD // 2)
        qs_o[:, sl] = (qh * (SWA_SCALE * LOG2E)).astype(BF16)
    for j in range(SWA_KV_HEADS):
        sl = slice(j * LANES, (j + 1) * LANES)
        kh = sk[:, sl]
        if latent:
            kh = _rope(kh, sc, ss1, ss2, SWA_HD // 2)
        ks_o[:, sl] = kh.astype(ks_o.dtype)
    vs_o[...] = _v_variants(sv).astype(BF16)
    if not latent:
        ckv_o[...] = ckv_n
        sv_o[...] = sv


def _inproj(x2d, mod_l, g1, wp, tables, *, latent, seq, tm):
    n = x2d.shape[0]
    nblk = n // tm
    per_seq = seq // tm if latent else 1
    rows = lambda w: pl.BlockSpec((tm, w), lambda i: (i, 0))
    if latent:
        mod_spec = pl.BlockSpec((1, N_MOD, D_MODEL), lambda i: (i // per_seq, 0, 0))
    else:
        mod_spec = pl.BlockSpec((1, N_MOD, D_MODEL), lambda i: (MOD_ROWS // 2, 0, 0))
    in_specs = [rows(D_MODEL), mod_spec, _resident((1, D_MODEL)), _resident((D_MODEL, IN_PACKED)),
                _resident((1, MLA_Q_LORA)), _resident((MLA_Q_LORA, 4 * LANES)),
                _resident((1, MLA_KV_LORA)), _resident((MLA_KV_LORA, 4 * LANES)),
                _resident((MLA_HEADS * MLA_V, MLA_KV_LORA))]
    args = [x2d, mod_l, g1, wp["w_in"], wp["gq"], wp["wuq"], wp["gkv"], wp["wuk"], wp["wuv"]]
    out_shape = [jax.ShapeDtypeStruct((n, 1536), F32), jax.ShapeDtypeStruct((n, 512), F32),
                 jax.ShapeDtypeStruct((n, LANES), F32),
                 jax.ShapeDtypeStruct((n, 512), BF16), jax.ShapeDtypeStruct((n, 512), BF16),
                 jax.ShapeDtypeStruct((MLA_HEADS * VT_ROWS, n), BF16),
                 jax.ShapeDtypeStruct((n, 512), BF16),
                 jax.ShapeDtypeStruct((n, 256), BF16 if latent else F32),
                 jax.ShapeDtypeStruct((n, 512), BF16)]
    out_specs = [rows(1536), rows(512), rows(LANES), rows(512), rows(512),
                 pl.BlockSpec((MLA_HEADS * VT_ROWS, tm), lambda i: (0, i)),
                 rows(512), rows(256), rows(512)]
    if latent:
        tab = pl.BlockSpec((tm, LANES), lambda i: (i % per_seq, 0))
        in_specs += [tab] * 6
        args += list(tables)
    else:
        out_shape += [jax.ShapeDtypeStruct((n, LANES), F32), jax.ShapeDtypeStruct((n, LANES), F32)]
        out_specs += [rows(LANES), rows(LANES)]
    return pl.pallas_call(
        functools.partial(_inproj_kernel, latent=latent),
        grid=(nblk,), in_specs=in_specs, out_specs=out_specs, out_shape=out_shape,
        compiler_params=_params(1), name="inproj_lat" if latent else "inproj_ctx",
    )(*args)


def _cache_kernel(ckv_ref, kr_ref, sk_ref, sv_ref, wuk_ref, wuv_ref, km_o, vm_o, ks_o, vs_o):
    ckv = ckv_ref[0, 0].astype(BF16)
    kn = _dot(ckv, wuk_ref[...])
    kr = kr_ref[0, 0]
    for h in range(MLA_HEADS):
        sl = slice(h * LANES, (h + 1) * LANES)
        km_o[0, :, sl] = (kn[:, sl] + kr).astype(BF16)
    _store_vt(vm_o.at[0], wuv_ref[...], ckv)
    sk = sk_ref[0, 0]
    lane = lax.broadcasted_iota(jnp.int32, sk.shape, 1)
    lo = lane < SWA_HD
    ks_o[0, :, 0:LANES] = jnp.where(lo, sk, 0.0).astype(BF16)
    ks_o[0, :, LANES:2 * LANES] = jnp.where(lo, pltpu.roll(sk, SWA_HD, 1), 0.0).astype(BF16)
    vs_o[0] = _v_variants(sv_ref[0, 0]).astype(BF16)


def _cache_prep(layer, ckv, kr_pad, sk, sv, wp):
    b, _, p, _ = ckv.shape
    cin = pl.BlockSpec((1, 1, p, LANES), lambda i: (i, layer, 0, 0))
    out = lambda w: pl.BlockSpec((1, p, w), lambda i: (i, 0, 0))
    return pl.pallas_call(
        _cache_kernel, grid=(b,),
        in_specs=[cin, cin, cin, cin, _resident((MLA_KV_LORA, 4 * LANES)),
                  _resident((MLA_HEADS * MLA_V, MLA_KV_LORA))],
        out_specs=[out(512), pl.BlockSpec((1, MLA_HEADS * VT_ROWS, p), lambda i: (i, 0, 0)), out(256), out(512)],
        out_shape=[jax.ShapeDtypeStruct((b, p, 512), BF16),
                   jax.ShapeDtypeStruct((b, MLA_HEADS * VT_ROWS, p), BF16),
                   jax.ShapeDtypeStruct((b, p, 256), BF16), jax.ShapeDtypeStruct((b, p, 512), BF16)],
        compiler_params=_params(1), name="cache_prep",
    )(ckv, kr_pad, sk, sv, wp["wuk"], wp["wuv"])


PAIR = 2 * DN_CHUNK


def _bmm(a, b):
    return jnp.einsum("bij,bjk->bik", a, b, preferred_element_type=F32)


def _bmm_nt(a, b):
    return jnp.einsum("bik,bjk->bij", a, b, preferred_element_type=F32)


def _split(a):
    hi = a.astype(BF16)
    return hi, (a - hi.astype(F32)).astype(BF16)


def _mm_split(a2, b):
    c = a2.shape[1]
    r = _bmm(jnp.concatenate(_split(a2), axis=1), jnp.concatenate(_split(b), axis=1))
    return r[:, :c] + r[:, c:]


def _tri_inverse_dup(low2):
    c, n2 = low2.shape[1], low2.shape[2]
    eye2 = (lax.broadcasted_iota(jnp.int32, (c, n2), 0)
            == (lax.broadcasted_iota(jnp.int32, (c, n2), 1) & (c - 1))).astype(F32)
    pp = -low2
    tt = eye2 + pp
    pp = _mm_split(pp, pp)
    s = 2
    while 2 * s < c:
        r = _mm_split(pp, jnp.concatenate([pp, tt], axis=2))
        pp = r[:, :, :n2]
        tt = tt + r[:, :, n2:]
        s *= 2
    return tt + _mm_split(pp, tt)


def _chunk_cumsum(x, forward):
    rows = x.shape[0]
    rc = lax.broadcasted_iota(jnp.int32, x.shape, 0) & (DN_CHUNK - 1)
    s = 1
    while s < DN_CHUNK:
        if forward:
            x = x + jnp.where(rc >= s, pltpu.roll(x, s, 0), 0.0)
        else:
            x = x + jnp.where(rc < DN_CHUNK - s, pltpu.roll(x, rows - s, 0), 0.0)
        s *= 2
    return x


def _dn_prep_kernel(qkv_ref, prev_ref, next_ref, small_ref, cw_ref, alog_ref, dtb_ref,
                    u_o, w_o, qg_o, kdt_o, intra_o, eg_o, *, tt, tiles_per_seq):
    C = DN_CHUNK
    i = pl.program_id(0)
    first = (i % tiles_per_seq) == 0
    last = (i % tiles_per_seq) == tiles_per_seq - 1
    row = lax.broadcasted_iota(jnp.int32, (tt, LANES), 0)
    n_chunks = tt // C
    hw = DN_HEADS * LANES

    def conv(c0):
        sl = slice(c0, c0 + LANES)
        x = qkv_ref[:, sl]
        prev_row = jnp.where(first, 0.0, prev_ref[7:8, sl])
        next_row = jnp.where(last, 0.0, next_ref[0:1, sl])
        xp = jnp.where(row == 0, prev_row, pltpu.roll(x, 1, 0))
        xn = jnp.where(row == tt - 1, next_row, pltpu.roll(x, tt - 1, 0))
        y = xp * cw_ref[0:1, sl] + x * cw_ref[1:2, sl] + xn * cw_ref[2:3, sl]
        return y * _sigmoid(y)

    small = small_ref[...]
    ri = lax.broadcasted_iota(jnp.int32, (C, LANES), 0)
    cj = lax.broadcasted_iota(jnp.int32, (C, LANES), 1)
    cm = cj & (C - 1)
    incl = (ri >= cm, ri <= cm)
    strict = (ri > cm, ri < cm)
    half = (cj < C, cj >= C)

    qs, ks, vs = [], [], []
    for j in range(DN_HEADS):
        q = conv(j * LANES)
        k = conv(hw + j * LANES)
        vs.append(conv(2 * hw + j * LANES))
        qs.append(q * lax.rsqrt(jnp.sum(q * q, -1, keepdims=True) + EPS) * (DN_DK ** -0.5))
        ks.append(k * lax.rsqrt(jnp.sum(k * k, -1, keepdims=True) + EPS))

    lane_t = lax.broadcasted_iota(jnp.int32, (tt, LANES), 1)
    xa = small + dtb_ref[...]
    softplus = jnp.maximum(xa, 0.0) + jnp.log(1.0 + jnp.exp(-jnp.abs(xa)))
    g_all = -jnp.exp(alog_ref[...]) * softplus
    beta_all = _sigmoid(small)
    gc_all = jnp.where(lane_t < DN_HEADS, _chunk_cumsum(g_all, True), _chunk_cumsum(g_all, False))
    gc_rows = gc_all.T
    left = lax.broadcasted_iota(jnp.int32, (1, LANES), 1) < C

    ch = lambda a, c: a[c * C:(c + 1) * C]
    hc = [(j, c) for j in range(DN_HEADS) for c in range(n_chunks)]
    kq = jnp.stack([jnp.concatenate([ch(ks[j], c), ch(qs[j], c)], axis=0).astype(BF16) for j, c in hc])
    kk = jnp.stack([jnp.concatenate([ch(ks[j], c)] * 2, axis=0).astype(BF16) for j, c in hc])
    gq = _bmm_nt(kq, kk)

    chains = [(j, c, d) for j, c in hc for d in range(2)]
    lows, rhss, decays, gcs = [], [], [], []
    for j, c, d in chains:
        b = hc.index((j, c))
        ln = DN_HEADS * d + j
        gc = jnp.broadcast_to(ch(gc_all, c)[:, ln:ln + 1], (C, LANES))
        beta = ch(beta_all, c)[:, 2 * DN_HEADS + ln:2 * DN_HEADS + ln + 1]
        pair_row = gc_rows[ln:ln + 1, (c // 2) * PAIR:(c // 2 + 1) * PAIR]
        swapped = pltpu.roll(pair_row, C, 1)
        key_row = jnp.where(left, pair_row, swapped) if c % 2 == 0 else jnp.where(left, swapped, pair_row)
        decay = jnp.exp(jnp.where(incl[d], gc - key_row, NEG_BIG))
        lows.append(jnp.where(strict[d], gq[b, :C] * beta * decay, 0.0))
        kc = ch(ks[j], c)
        rhss.append(jnp.concatenate([ch(vs[j], c) * beta, kc * beta * jnp.exp(gc)], axis=1).astype(BF16))
        decays.append(decay)
        gcs.append(gc)
    tinv = _tri_inverse_dup(jnp.stack(lows))
    th, tl = _split(tinv[:, :, :C])
    sol = _bmm(jnp.concatenate([th, tl], axis=1), jnp.stack(rhss))
    sol = sol[:, :C] + sol[:, C:]

    kdec = {}
    for n, (j, c, d) in enumerate(chains):
        b = hc.index((j, c))
        rows = slice(c * C, (c + 1) * C)
        cols = slice(j * LANES, (j + 1) * LANES)
        gc = gcs[n]
        u_o[d, rows, cols] = sol[n, :, :DN_DV]
        w_o[d, rows, cols] = sol[n, :, DN_DV:].astype(BF16)
        intra_o[d, rows, cols] = jnp.where(incl[d] & half[c % 2], gq[b, C:] * decays[n], 0.0).astype(BF16)
        qg_o[d, rows, cols] = (ch(qs[j], c) * jnp.exp(gc)).astype(BF16)
        g_last = gc[C - 1:C, :] if d == 0 else gc[0:1, :]
        kdec[(j, c, d)] = ch(ks[j], c) * jnp.exp(g_last - gc)
        eg_o[d, c * 8:(c + 1) * 8, cols] = jnp.broadcast_to(jnp.exp(g_last), (8, LANES))
    for j in range(DN_HEADS):
        for m in range(n_chunks // 2):
            for d in range(2):
                slab = jnp.concatenate([kdec[(j, 2 * m, d)], kdec[(j, 2 * m + 1, d)]], axis=0)
                kdt_o[d, m * PAIR:(m + 1) * PAIR, j * LANES:(j + 1) * LANES] = slab.T.astype(BF16)


def _dn_prep(qkv, small, conv_w, a_log, dt_bias, *, seq):
    n = qkv.shape[0]
    tt = 256
    t8 = tt // 8
    last8 = n // 8 - 1
    hw = DN_HEADS * LANES
    lane_vec = lambda a: jnp.pad(a.reshape(1, 2 * DN_HEADS), ((0, 0), (0, LANES - 2 * DN_HEADS)))
    out = lambda rows: pl.BlockSpec((2, rows, hw), lambda i: (0, i, 0))
    big = lambda dt: jax.ShapeDtypeStruct((2, n, hw), dt)
    return pl.pallas_call(
        functools.partial(_dn_prep_kernel, tt=tt, tiles_per_seq=seq // tt),
        grid=(n // tt,),
        in_specs=[pl.BlockSpec((tt, 3 * hw), lambda i: (i, 0)),
                  pl.BlockSpec((8, 3 * hw), lambda i: (jnp.maximum(i * t8 - 1, 0), 0)),
                  pl.BlockSpec((8, 3 * hw), lambda i: (jnp.minimum((i + 1) * t8, last8), 0)),
                  pl.BlockSpec((tt, LANES), lambda i: (i, 0)),
                  _resident((3, 3 * hw)), _resident((1, LANES)), _resident((1, LANES))],
        out_specs=[out(tt)] * 5 + [out(t8)],
        out_shape=[big(F32), big(BF16), big(BF16), big(BF16), big(BF16),
                   jax.ShapeDtypeStruct((2, n // 8, hw), F32)],
        compiler_params=_params(1), name="dn_prep",
    )(qkv, qkv, qkv, small, conv_w, lane_vec(a_log), lane_vec(dt_bias))


def _dn_scan_kernel(*refs, ts, bb, has_s0, want_state):
    it = iter(refs)
    ops = [[next(it) for _ in range(6)] for _ in range(2)]
    s0_ref = next(it) if has_s0 else None
    o_refs = [next(it), next(it)]
    sfin_ref = next(it) if want_state else None
    st = next(it)
    i = pl.program_id(1)
    chains = [(g, d, h) for g in range(bb) for d in range(2) for h in range(DN_HEADS)]

    @pl.when(i == 0)
    def _():
        for n, (g, d, h) in enumerate(chains):
            st[n] = s0_ref[g, d, h] if has_s0 else jnp.zeros((DN_DK, DN_DV), F32)

    n_chunks = ts // DN_CHUNK
    zeros = jnp.zeros((DN_CHUNK, DN_DV), F32)
    S = st[...]
    for step in range(n_chunks):
        chunk = lambda d: step if d == 0 else n_chunks - 1 - step
        wq, us, intras, kdts, egs = [], [], [], [], []
        for g, d, h in chains:
            u_ref, w_ref, qg_ref, kdt_ref, intra_ref, eg_ref = ops[d]
            c = chunk(d)
            rows = slice(c * DN_CHUNK, (c + 1) * DN_CHUNK)
            prow = slice((c // 2) * PAIR, (c // 2 + 1) * PAIR)
            cols = slice(h * LANES, (h + 1) * LANES)
            wq.append(jnp.concatenate([w_ref[0, g, rows, cols], qg_ref[0, g, rows, cols]], axis=0))
            us.append(u_ref[0, g, rows, cols])
            intras.append(intra_ref[0, g, rows, cols])
            kdts.append(kdt_ref[0, g, prow, cols])
            egs.append(eg_ref[0, g, c * 8:c * 8 + 1, cols])
        r = _bmm(jnp.stack(wq), S.astype(BF16))
        v_new = jnp.stack(us) - r[:, :DN_CHUNK]
        v_ext = jnp.stack([jnp.concatenate([v_new[n], zeros] if chunk(d) % 2 == 0 else [zeros, v_new[n]], axis=0)
                           for n, (g, d, h) in enumerate(chains)]).astype(BF16)
        o = r[:, DN_CHUNK:] + _bmm(jnp.stack(intras), v_ext)
        S = S * jnp.stack(egs) + _bmm(jnp.stack(kdts), v_ext)
        for n, (g, d, h) in enumerate(chains):
            c = chunk(d)
            o_refs[d][g, c * DN_CHUNK:(c + 1) * DN_CHUNK, h * LANES:(h + 1) * LANES] = o[n]
    st[...] = S

    if want_state:
        @pl.when(i == pl.num_programs(1) - 1)
        def _():
            for n, (g, d, h) in enumerate(chains):
                sfin_ref[g, d, h] = S[n]


def _dn_scan(prep, s0, *, batch, seq, want_state):
    ts = min(seq, 512)
    nt = seq // ts
    bb = 2
    hw = DN_HEADS * LANES
    in_specs, args = [], []
    for d in range(2):
        imap = (lambda b, i: (0, b, i, 0)) if d == 0 else (lambda b, i: (1, b, nt - 1 - i, 0))
        for a in prep:
            rows = a.shape[1] // batch
            in_specs.append(pl.BlockSpec((1, bb, rows // nt, hw), imap))
            args.append(a.reshape(2, batch, rows, hw))
    state_spec = pl.BlockSpec((bb, 2, DN_HEADS, DN_DK, DN_DV), lambda b, i: (b, 0, 0, 0, 0))
    if s0 is not None:
        in_specs.append(state_spec)
        args.append(s0)
    out_specs = [pl.BlockSpec((bb, ts, hw), lambda b, i: (b, i, 0)),
                 pl.BlockSpec((bb, ts, hw), lambda b, i: (b, nt - 1 - i, 0))]
    out_shape = [jax.ShapeDtypeStruct((batch, seq, hw), F32)] * 2
    if want_state:
        out_specs.append(state_spec)
        out_shape.append(jax.ShapeDtypeStruct((batch, 2, DN_HEADS, DN_DK, DN_DV), F32))
    res = pl.pallas_call(
        functools.partial(_dn_scan_kernel, ts=ts, bb=bb, has_s0=s0 is not None, want_state=want_state),
        grid=(batch // bb, nt), in_specs=in_specs, out_specs=out_specs, out_shape=out_shape,
        scratch_shapes=[pltpu.VMEM((2 * bb * DN_HEADS, DN_DK, DN_DV), F32)],
        compiler_params=_params(2), name="dn_scan",
    )(*args)
    return (res[0].reshape(batch * seq, hw), res[1].reshape(batch * seq, hw), res[2] if want_state else None)


def _swa_kernel(*refs, tq, seq, windowed, ctx_len):
    it = iter(refs)
    q_ref, k_ref, v_ref = next(it), next(it), next(it)
    kc_ref, vc_ref = (next(it), next(it)) if ctx_len else (None, None)
    sink_ref = next(it)
    o_ref = next(it)
    group = SWA_HEADS // SWA_KV_HEADS
    q0 = pl.program_id(1) * tq
    if windowed:
        wl = tq + 2 * WINDOW
        ws = pl.multiple_of(jnp.clip(q0 - WINDOW, 0, seq - wl), WINDOW)
        qpos = q0 + lax.broadcasted_iota(jnp.int32, (tq, wl), 0)
        kpos = ws + lax.broadcasted_iota(jnp.int32, (tq, wl), 1)
        valid = jnp.abs(qpos - kpos) <= WINDOW
        rows = pl.ds(ws, wl)
    else:
        rows = slice(None)
    def scores(h):
        kh = h // group
        qh = q_ref[0, :, h * LANES:(h + 1) * LANES]
        s = _dot_nt(qh, k_ref[0, rows, kh * LANES:(kh + 1) * LANES].astype(BF16))
        if windowed:
            s = jnp.where(valid, s, NEG_BIG)
        sc = _dot_nt(qh, kc_ref[0, :, kh * LANES:(kh + 1) * LANES]) if ctx_len else None
        return h, s, sc

    def values(h, s, sc):
        sink = sink_ref[h] * LOG2E
        m = jnp.maximum(jnp.max(s, axis=-1, keepdims=True), sink)
        if ctx_len:
            m = jnp.maximum(m, jnp.max(sc, axis=-1, keepdims=True))
        p = jnp.exp2(s - m)
        l = jnp.sum(p, axis=-1, keepdims=True) + jnp.exp2(sink - m)
        pv = _dot(p.astype(BF16), v_ref[0, rows, h * LANES:(h + 1) * LANES])
        if ctx_len:
            pc = jnp.exp2(sc - m)
            l = l + jnp.sum(pc, axis=-1, keepdims=True)
            pv = pv + _dot(pc.astype(BF16), vc_ref[0, :, h * LANES:(h + 1) * LANES])
        return pv / l

    outs, pending = [], None
    for h in range(SWA_HEADS):
        cur = scores(h)
        if pending is not None:
            outs.append(values(*pending))
        pending = cur
    outs.append(values(*pending))
    for pair in range(2):
        o_ref[0, :, pair * LANES:(pair + 1) * LANES] = (outs[2 * pair] + outs[2 * pair + 1]).astype(o_ref.dtype)


def _swa_attention(q, k, v, kc, vc, sink, *, windowed, tq):
    b, seq, _ = q.shape
    ctx_len = 0 if kc is None else kc.shape[1]
    full = lambda a: pl.BlockSpec((1,) + a.shape[1:], lambda i, j: (i, 0, 0))
    in_specs = [pl.BlockSpec((1, tq, 4 * LANES), lambda i, j: (i, j, 0)), full(k), full(v)]
    args = [q, k, v]
    if ctx_len:
        in_specs += [full(kc), full(vc)]
        args += [kc, vc]
    in_specs.append(pl.BlockSpec(memory_space=pltpu.SMEM))
    args.append(sink)
    return pl.pallas_call(
        functools.partial(_swa_kernel, tq=tq, seq=seq, windowed=windowed, ctx_len=ctx_len),
        grid=(b, seq // tq), in_specs=in_specs,
        out_specs=pl.BlockSpec((1, tq, 2 * LANES), lambda i, j: (i, j, 0)),
        out_shape=jax.ShapeDtypeStruct((b, seq, 2 * LANES), BF16),
        compiler_params=_params(2), name="swa_attn",
    )(*args)


def _mla_kernel(*refs, has_ctx):
    if has_ctx:
        q_ref, k_ref, vt_ref, kc_ref, vtc_ref, o_ref = refs
    else:
        q_ref, k_ref, vt_ref, o_ref = refs
    seq = k_ref.shape[1]
    kb = min(seq, 512)
    blocks = [(lambda sl, s=s: k_ref[0, s:s + kb, sl], lambda vr, s=s: vt_ref[vr, s:s + kb])
              for s in range(0, seq, kb)]
    if has_ctx:
        blocks.append((lambda sl: kc_ref[0, :, sl], lambda vr: vtc_ref[0, vr, :]))

    def scores(h):
        sl = slice(h * LANES, (h + 1) * LANES)
        qh = q_ref[0, :, sl]
        sts = [_dot_nt(keys(sl), qh) for keys, _ in blocks]
        m = functools.reduce(jnp.maximum, [jnp.max(st, axis=0, keepdims=True) for st in sts])
        return h, sts, m

    def values(h, sts, m):
        vr = slice(h * VT_ROWS, (h + 1) * VT_ROWS)
        r = None
        for st, (_, vals) in zip(sts, blocks):
            part = _dot(vals(vr), jnp.exp2(st - m).astype(BF16))
            r = part if r is None else r + part
        return r[:MLA_V] / r[MLA_V:MLA_V + 1]

    outs, pending = [], None
    for h in range(MLA_HEADS):
        cur = scores(h)
        if pending is not None:
            outs.append(values(*pending))
        pending = cur
    outs.append(values(*pending))
    o_ref[0] = jnp.concatenate(outs, axis=0).T.astype(o_ref.dtype)


def _mla_attention(q, k, vt, kc, vtc, *, tq):
    b, seq, _ = q.shape
    in_specs = [pl.BlockSpec((1, tq, 4 * LANES), lambda i, j: (i, j, 0)),
                pl.BlockSpec((1, seq, 4 * LANES), lambda i, j: (i, 0, 0)),
                pl.BlockSpec((MLA_HEADS * VT_ROWS, seq), lambda i, j: (0, i))]
    args = [q, k, vt]
    if kc is not None:
        p = kc.shape[1]
        in_specs += [pl.BlockSpec((1, p, 4 * LANES), lambda i, j: (i, 0, 0)),
                     pl.BlockSpec((1, MLA_HEADS * VT_ROWS, p), lambda i, j: (i, 0, 0))]
        args += [kc, vtc]
    return pl.pallas_call(
        functools.partial(_mla_kernel, has_ctx=kc is not None),
        grid=(b, seq // tq), in_specs=in_specs,
        out_specs=pl.BlockSpec((1, tq, MLA_HEADS * MLA_V), lambda i, j: (i, j, 0)),
        out_shape=jax.ShapeDtypeStruct((b, seq, MLA_HEADS * MLA_V), BF16),
        compiler_params=_params(2), name="mla_attn",
    )(*args)


def _ffn_kernel(*refs, final):
    if final:
        (x_ref, of_ref, obw_ref, z_ref, ng_ref, ob_ref, oc_ref, mod_ref, n2_ref, wo_ref, w1_ref, w2_ref,
         fg_ref, o_ref) = refs
    else:
        (x_ref, of_ref, obw_ref, z_ref, ng_ref, ob_ref, oc_ref, mod_ref, n2_ref, wo_ref, w1_ref, w2_ref,
         o_ref) = refs
    g1 = mod_ref[0, 2:3, :]
    sh2 = mod_ref[0, 3:4, :]
    sc2 = mod_ref[0, 4:5, :]
    g2 = mod_ref[0, 5:6, :]
    mix = _dot(ob_ref[...], wo_ref[512:768, :]) + _dot(oc_ref[...], wo_ref[768:1024, :])
    for h in range(DN_HEADS):
        sl = slice(h * LANES, (h + 1) * LANES)
        z = z_ref[:, sl]
        oa = _rms(of_ref[:, sl] + obw_ref[:, sl], ng_ref[...]) * (z * _sigmoid(z))
        mix = mix + _dot(oa.astype(BF16), wo_ref[sl, :])
    x = x_ref[...] + g1 * mix
    h2 = (_rms(x, n2_ref[...]) * (1.0 + sc2) + sh2).astype(BF16)
    acc = None
    fc = 1024
    for c in range(D_FF // fc):
        a = jnp.maximum(_dot(h2, w1_ref[:, c * fc:(c + 1) * fc]), 0.0)
        part = _dot((a * a).astype(BF16), w2_ref[c * fc:(c + 1) * fc, :])
        acc = part if acc is None else acc + part
    x = x + g2 * acc
    if final:
        x = _rms(x, fg_ref[...])
    o_ref[...] = x


def _ffn(x2d, o_fwd, o_bwd, z, dn_g, ob, oc, mod_l, n2g, wp, final_g, *, latent, seq, tm):
    n = x2d.shape[0]
    per_seq = seq // tm if latent else 1
    rows = lambda w: pl.BlockSpec((tm, w), lambda i: (i, 0))
    if latent:
        mod_spec = pl.BlockSpec((1, N_MOD, D_MODEL), lambda i: (i // per_seq, 0, 0))
    else:
        mod_spec = pl.BlockSpec((1, N_MOD, D_MODEL), lambda i: (MOD_ROWS // 2, 0, 0))
    in_specs = [rows(D_MODEL), rows(512), rows(512), rows(512), _resident((1, DN_DV)), rows(256), rows(256),
                mod_spec, _resident((1, D_MODEL)),
                _resident((D_MODEL, D_MODEL)), _resident((D_MODEL, D_FF)), _resident((D_FF, D_MODEL))]
    args = [x2d, o_fwd, o_bwd, z, dn_g, ob, oc, mod_l, n2g, wp["w_out"], wp["w_ff1"], wp["w_ff2"]]
    if final_g is not None:
        in_specs.append(_resident((1, D_MODEL)))
        args.append(final_g)
    return pl.pallas_call(
        functools.partial(_ffn_kernel, final=final_g is not None),
        grid=(n // tm,), in_specs=in_specs, out_specs=rows(D_MODEL),
        out_shape=jax.ShapeDtypeStruct((n, D_MODEL), F32),
        compiler_params=_params(1), name="ffn",
    )(*args)


def _rope_tables(n_tok, rot_dim, lane0):
    rows = n_tok // GRID_W
    row = jnp.repeat(jnp.arange(rows, dtype=F32), GRID_W)
    colp = jnp.tile(jnp.arange(GRID_W, dtype=F32), rows)
    n_freq = rot_dim // 4
    inv_freq = ROPE_BASE ** (-jnp.arange(n_freq, dtype=F32) / n_freq)
    ang = jnp.concatenate([row[:, None] * inv_freq, colp[:, None] * inv_freq], axis=-1)
    cos, sin = jnp.cos(ang), jnp.sin(ang)
    half = rot_dim // 2
    zeros = lambda w: jnp.zeros((n_tok, w), F32)
    tail = LANES - lane0 - rot_dim
    c = jnp.concatenate([jnp.ones((n_tok, lane0), F32), cos, cos, zeros(tail)], axis=1)
    s1 = jnp.concatenate([zeros(lane0 + half), sin, zeros(tail)], axis=1)
    s2 = jnp.concatenate([zeros(lane0), -sin, zeros(half + tail)], axis=1)
    return c, s1, s2


def _pad_heads(w, n_heads, width):
    k = w.shape[0]
    w = w.reshape(k, n_heads, width)
    return jnp.pad(w, ((0, 0), (0, 0), (0, LANES - width))).reshape(k, n_heads * LANES)


def _pack_layer(l, w_in, mla_q_norm_g, mla_w_uq, mla_kv_norm_g, mla_w_ukv, w_out, w_ff1, w_ff2):
    w = w_in[l]
    offs = np.cumsum([0, 1536, 512, 4, 4, 4, 4, MLA_Q_LORA, MLA_KV_LORA, MLA_ROPE, 256, 128, 128])
    piece = lambda i: w[:, offs[i]:offs[i + 1]]
    k = w.shape[0]
    small = jnp.concatenate([piece(2), piece(3), piece(4), piece(5), jnp.zeros((k, KR_LANE - 16), F32),
                             piece(8), jnp.zeros((k, LANES - KR_LANE - MLA_ROPE), F32)], axis=1)
    w_packed = jnp.concatenate([piece(0), piece(1), small, piece(6), piece(7),
                                _pad_heads(piece(9), SWA_HEADS, SWA_HD),
                                _pad_heads(piece(10), SWA_KV_HEADS, SWA_HD), piece(11)], axis=1)
    ukv = mla_w_ukv[l].reshape(MLA_KV_LORA, MLA_HEADS, MLA_NOPE + MLA_V)
    wuk = jnp.pad(ukv[:, :, :MLA_NOPE], ((0, 0), (0, 0), (0, LANES - MLA_NOPE)))
    wuv_t = ukv[:, :, MLA_NOPE:].reshape(MLA_KV_LORA, MLA_HEADS * MLA_V).T
    return dict(
        w_in=w_packed.astype(BF16),
        gq=mla_q_norm_g[l][None, :], gkv=mla_kv_norm_g[l][None, :],
        wuq=_pad_heads(mla_w_uq[l], MLA_HEADS, MLA_NOPE + MLA_ROPE).astype(BF16),
        wuk=wuk.reshape(MLA_KV_LORA, MLA_HEADS * LANES).astype(BF16),
        wuv=wuv_t.astype(BF16),
        w_out=w_out[l].astype(BF16), w_ff1=w_ff1[l].astype(BF16), w_ff2=w_ff2[l].astype(BF16))


def _layer(x2d, mod_l, wp, lp, ctx, final_g, *, batch, seq, latent, tables):
    tm = 512
    outs = _inproj(x2d, mod_l, lp["norm1_g"], wp, tables, latent=latent, seq=seq, tm=tm)
    qkv, z, small, qm, km, vm, qs, ks, vs = outs[:9]
    r3 = lambda a: a.reshape(batch, seq, a.shape[-1])
    prep = _dn_prep(qkv, small, lp["dn_conv_w"], lp["dn_a_log"], lp["dn_dt_bias"], seq=seq)
    o_fwd, o_bwd, state = _dn_scan(prep, ctx["dn"] if latent else None, batch=batch, seq=seq,
                                   want_state=not latent)
    tq = 256
    if latent:
        o_b = _mla_attention(r3(qm), r3(km), vm, ctx["km"], ctx["vm"], tq=tq)
        o_c = _swa_attention(r3(qs), r3(ks), r3(vs), ctx["ks"], ctx["vs"], lp["swa_sink"],
                             windowed=True, tq=tq)
    else:
        o_b = _mla_attention(r3(qm), r3(km), vm, None, None, tq=tq)
        o_c = _swa_attention(r3(qs), r3(ks), r3(vs), None, None, lp["swa_sink"], windowed=False, tq=tq)
    n = batch * seq
    x_new = _ffn(x2d, o_fwd, o_bwd, z, lp["dn_norm_g"], o_b.reshape(n, -1), o_c.reshape(n, -1), mod_l,
                 lp["norm2_g"], wp, final_g, latent=latent, seq=seq, tm=tm)
    if latent:
        return x_new, None
    ckv_n, sv = outs[9], outs[10]
    k_c = ks.reshape(batch, seq, SWA_KV_HEADS, LANES)[..., :SWA_HD]
    new = (state, ckv_n.reshape(batch, seq, MLA_KV_LORA),
           small[:, KR_LANE:KR_LANE + MLA_ROPE].reshape(batch, seq, MLA_ROPE),
           k_c, sv.reshape(batch, seq, SWA_KV_HEADS, SWA_HD))
    return x_new, new


def kernel(x_prompt, x_sample, state_dn, cache_mla_ckv, cache_mla_krope, cache_swa_k, cache_swa_v,
           c, c_ctx, ada_w, ada_b, norm1_g, norm2_g, w_in, dn_conv_w, dn_a_log, dn_dt_bias, dn_norm_g,
           mla_q_norm_g, mla_w_uq, mla_kv_norm_g, mla_w_ukv, swa_sink, w_out, w_ff1, w_ff2, final_norm_g):
    depth = ada_w.shape[0]
    bp, sp, _ = x_prompt.shape
    bs, ss, _ = x_sample.shape
    p_len = cache_mla_ckv.shape[2]
    assert bs <= MOD_ROWS // 2

    cvec = jnp.zeros((MOD_ROWS, D_MODEL), F32).at[:bs].set(c).at[MOD_ROWS // 2].set(c_ctx)
    mod = _modulation(cvec, ada_w, ada_b).reshape(depth, MOD_ROWS, N_MOD, D_MODEL)

    packed = [_pack_layer(l, w_in, mla_q_norm_g, mla_w_uq, mla_kv_norm_g, mla_w_ukv, w_out, w_ff1, w_ff2)
              for l in range(depth)]
    lps = [dict(norm1_g=norm1_g[l][None, :], norm2_g=norm2_g[l][None, :], dn_conv_w=dn_conv_w[l],
                dn_a_log=dn_a_log[l], dn_dt_bias=dn_dt_bias[l], dn_norm_g=dn_norm_g[l][None, :],
                swa_sink=swa_sink[l]) for l in range(depth)]
    final_g = final_norm_g[None, :]

    xp = x_prompt.reshape(bp * sp, D_MODEL)
    news = []
    for l in range(depth):
        xp, new = _layer(xp, mod[l], packed[l], lps[l], None, final_g if l == depth - 1 else None,
                         batch=bp, seq=sp, latent=False, tables=None)
        news.append(new)
    y_prompt = xp.reshape(bp, sp, D_MODEL)

    tables = _rope_tables(ss, MLA_ROPE, KR_LANE) + _rope_tables(ss, SWA_HD, 0)
    kr_pad = jnp.pad(cache_mla_krope, ((0, 0), (0, 0), (0, 0), (KR_LANE, LANES - KR_LANE - MLA_ROPE)))
    sk_c = cache_swa_k.reshape(bs, depth, p_len, SWA_KV_HEADS * SWA_HD)
    sv_c = cache_swa_v.reshape(bs, depth, p_len, SWA_KV_HEADS * SWA_HD)
    xs = x_sample.reshape(bs * ss, D_MODEL)
    for l in range(depth):
        km, vm, ks, vs = _cache_prep(l, cache_mla_ckv, kr_pad, sk_c, sv_c, packed[l])
        ctx = dict(dn=state_dn[:, l], km=km, vm=vm, ks=ks, vs=vs)
        xs, _ = _layer(xs, mod[l], packed[l], lps[l], ctx, final_g if l == depth - 1 else None,
                       batch=bs, seq=ss, latent=True, tables=tables)
    y_sample = xs.reshape(bs, ss, D_MODEL)

    stack = lambda i: jnp.stack([nw[i] for nw in news], axis=1)
    return (y_prompt, y_sample, stack(0), stack(1), stack(2), stack(3), stack(4))
```

```python
import functools

import numpy as np
import jax
import jax.numpy as jnp
from jax import lax
from jax.experimental import pallas as pl
from jax.experimental.pallas import tpu as pltpu

F32 = jnp.float32
BF16 = jnp.bfloat16

D_MODEL = 1024
GRID_W = 64
EPS = 1e-6
ROPE_BASE = 10000.0
DN_HEADS = 4
DN_DK = 128
DN_DV = 128
DN_CHUNK = 64
MLA_HEADS = 4
MLA_Q_LORA = 256
MLA_KV_LORA = 128
MLA_NOPE = 64
MLA_ROPE = 32
MLA_V = 64
SWA_HEADS = 4
SWA_KV_HEADS = 2
SWA_HD = 64
WINDOW = 128
D_FF = 4 * D_MODEL
N_MOD = 6
LOG2E = 1.4426950408889634
MLA_SCALE = (MLA_NOPE + MLA_ROPE) ** -0.5
SWA_SCALE = SWA_HD ** -0.5

LANES = 128
MOD_ROWS = 16
NEG_BIG = -1e30
VMEM_LIMIT = 56 * 1024 * 1024

C_QKV = (0, 1536)
C_Z = (1536, 2048)
C_SMALL = (2048, 2176)
C_CQ = (2176, 2432)
C_CKV = (2432, 2560)
C_SQ = (2560, 3072)
C_SK = (3072, 3328)
C_SV = (3328, 3456)
IN_PACKED = 3456
KR_LANE = MLA_NOPE
VT_ROWS = MLA_V + 16


def _params(n_grid):
    return pltpu.CompilerParams(dimension_semantics=("arbitrary",) * n_grid,
                                vmem_limit_bytes=VMEM_LIMIT)


def _resident(shape):
    nd = len(shape)
    return pl.BlockSpec(shape, lambda *_: (0,) * nd, pipeline_mode=pl.Buffered(1))


def _dot(a, b):
    return jnp.dot(a, b, preferred_element_type=F32)


def _dot_nt(a, b):
    return lax.dot_general(a, b, (((1,), (1,)), ((), ())), preferred_element_type=F32)


def _rms(x, g):
    return x * lax.rsqrt(jnp.mean(x * x, axis=-1, keepdims=True) + EPS) * g


def _sigmoid(x):
    return 1.0 / (1.0 + jnp.exp(-x))


def _rope(x, c, s1, s2, half):
    return x * c + pltpu.roll(x, half, 1) * s1 + pltpu.roll(x, LANES - half, 1) * s2


def _mod_kernel(c_ref, w_ref, b_ref, o_ref):
    c = c_ref[...]
    s = (c * _sigmoid(c)).astype(BF16)
    o_ref[0] = _dot(s, w_ref[0].astype(BF16)) + b_ref[0]


def _modulation(cvec, ada_w, ada_b):
    L = ada_w.shape[0]
    nb = 768
    return pl.pallas_call(
        _mod_kernel,
        grid=(L, N_MOD * D_MODEL // nb),
        in_specs=[pl.BlockSpec((MOD_ROWS, D_MODEL), lambda l, j: (0, 0)),
                  pl.BlockSpec((1, D_MODEL, nb), lambda l, j: (l, 0, j)),
                  pl.BlockSpec((1, 1, nb), lambda l, j: (l, 0, j))],
        out_specs=pl.BlockSpec((1, MOD_ROWS, nb), lambda l, j: (l, 0, j)),
        out_shape=jax.ShapeDtypeStruct((L, MOD_ROWS, N_MOD * D_MODEL), F32),
        compiler_params=_params(2),
        name="modulation",
    )(cvec, ada_w, ada_b.reshape(L, 1, N_MOD * D_MODEL))


def _v_variants(v):
    lane = lax.broadcasted_iota(jnp.int32, v.shape, 1)
    lo = lane < SWA_HD
    sw = pltpu.roll(v, SWA_HD, 1)
    zero = jnp.zeros_like(v)
    return jnp.concatenate([jnp.where(lo, v, zero), jnp.where(lo, zero, sw),
                            jnp.where(lo, sw, zero), jnp.where(lo, zero, v)], axis=1)


def _store_vt(vt_ref, wuv_t, ckv):
    vt = _dot_nt(wuv_t, ckv)
    ones = jnp.ones((VT_ROWS - MLA_V, vt.shape[1]), BF16)
    for h in range(MLA_HEADS):
        vt_ref[h * VT_ROWS:h * VT_ROWS + MLA_V, :] = vt[h * MLA_V:(h + 1) * MLA_V].astype(BF16)
        vt_ref[h * VT_ROWS + MLA_V:(h + 1) * VT_ROWS, :] = ones


def _inproj_kernel(*refs, latent):
    if latent:
        (x_ref, mod_ref, g1_ref, w_ref, gq_ref, wuq_ref, gkv_ref, wuk_ref, wuv_ref,
         mc_ref, ms1_ref, ms2_ref, sc_ref, ss1_ref, ss2_ref,
         qkv_o, z_o, small_o, qm_o, km_o, vm_o, qs_o, ks_o, vs_o) = refs
    else:
        (x_ref, mod_ref, g1_ref, w_ref, gq_ref, wuq_ref, gkv_ref, wuk_ref, wuv_ref,
         qkv_o, z_o, small_o, qm_o, km_o, vm_o, qs_o, ks_o, vs_o, ckv_o, sv_o) = refs
    x = x_ref[...]
    sh1 = mod_ref[0, 0:1, :]
    sc1 = mod_ref[0, 1:2, :]
    hb = (_rms(x, g1_ref[...]) * (1.0 + sc1) + sh1).astype(BF16)

    def proj(cols):
        return _dot(hb, w_ref[:, cols[0]:cols[1]])

    small = proj(C_SMALL)
    small_o[...] = small
    cq = proj(C_CQ)
    ckv = proj(C_CKV)
    sq = proj(C_SQ)
    sk = proj(C_SK)
    sv = proj(C_SV)

    cqn = _rms(cq, gq_ref[...]).astype(BF16)
    q = _dot(cqn, wuq_ref[...])
    ckv_n = _rms(ckv, gkv_ref[...])
    ckv_b = ckv_n.astype(BF16)
    kn = _dot(ckv_b, wuk_ref[...])
    _store_vt(vm_o, wuv_ref[...], ckv_b)
    qkv_o[...] = proj(C_QKV)
    z_o[...] = proj(C_Z)
    lane = lax.broadcasted_iota(jnp.int32, small.shape, 1)
    kr = jnp.where((lane >= KR_LANE) & (lane < KR_LANE + MLA_ROPE), small, 0.0)
    if latent:
        mc, ms1, ms2 = mc_ref[...], ms1_ref[...], ms2_ref[...]
        kr = _rope(kr, mc, ms1, ms2, MLA_ROPE // 2)
    for h in range(MLA_HEADS):
        sl = slice(h * LANES, (h + 1) * LANES)
        qh = q[:, sl]
        if latent:
            qh = _rope(qh, mc, ms1, ms2, MLA_ROPE // 2)
        qm_o[:, sl] = (qh * (MLA_SCALE * LOG2E)).astype(BF16)
        km_o[:, sl] = (kn[:, sl] + kr).astype(BF16)

    if latent:
        sc, ss1, ss2 = sc_ref[...], ss1_ref[...], ss2_ref[...]
    for h in range(SWA_HEADS):
        sl = slice(h * LANES, (h + 1) * LANES)
        qh = sq[:, sl]
        if latent:
            qh = _rope(qh, sc, ss1, ss2, SWA_HD // 2)
        qs_o[:, sl] = (qh * (SWA_SCALE * LOG2E)).astype(BF16)
    for j in range(SWA_KV_HEADS):
        sl = slice(j * LANES, (j + 1) * LANES)
        kh = sk[:, sl]
        if latent:
            kh = _rope(kh, sc, ss1, ss2, SWA_HD // 2)
        ks_o[:, sl] = kh.astype(ks_o.dtype)
    vs_o[...] = _v_variants(sv).astype(BF16)
    if not latent:
        ckv_o[...] = ckv_n
        sv_o[...] = sv


def _inproj(x2d, mod_l, g1, wp, tables, *, latent, seq, tm):
    n = x2d.shape[0]
    nblk = n // tm
    per_seq = seq // tm if latent else 1
    rows = lambda w: pl.BlockSpec((tm, w), lambda i: (i, 0))
    if latent:
        mod_spec = pl.BlockSpec((1, N_MOD, D_MODEL), lambda i: (i // per_seq, 0, 0))
    else:
        mod_spec = pl.BlockSpec((1, N_MOD, D_MODEL), lambda i: (MOD_ROWS // 2, 0, 0))
    in_specs = [rows(D_MODEL), mod_spec, _resident((1, D_MODEL)), _resident((D_MODEL, IN_PACKED)),
                _resident((1, MLA_Q_LORA)), _resident((MLA_Q_LORA, 4 * LANES)),
                _resident((1, MLA_KV_LORA)), _resident((MLA_KV_LORA, 4 * LANES)),
                _resident((MLA_HEADS * MLA_V, MLA_KV_LORA))]
    args = [x2d, mod_l, g1, wp["w_in"], wp["gq"], wp["wuq"], wp["gkv"], wp["wuk"], wp["wuv"]]
    out_shape = [jax.ShapeDtypeStruct((n, 1536), F32), jax.ShapeDtypeStruct((n, 512), F32),
                 jax.ShapeDtypeStruct((n, LANES), F32),
                 jax.ShapeDtypeStruct((n, 512), BF16), jax.ShapeDtypeStruct((n, 512), BF16),
                 jax.ShapeDtypeStruct((MLA_HEADS * VT_ROWS, n), BF16),
                 jax.ShapeDtypeStruct((n, 512), BF16),
                 jax.ShapeDtypeStruct((n, 256), BF16 if latent else F32),
                 jax.ShapeDtypeStruct((n, 512), BF16)]
    out_specs = [rows(1536), rows(512), rows(LANES), rows(512), rows(512),
                 pl.BlockSpec((MLA_HEADS * VT_ROWS, tm), lambda i: (0, i)),
                 rows(512), rows(256), rows(512)]
    if latent:
        tab = pl.BlockSpec((tm, LANES), lambda i: (i % per_seq, 0))
        in_specs += [tab] * 6
        args += list(tables)
    else:
        out_shape += [jax.ShapeDtypeStruct((n, LANES), F32), jax.ShapeDtypeStruct((n, LANES), F32)]
        out_specs += [rows(LANES), rows(LANES)]
    return pl.pallas_call(
        functools.partial(_inproj_kernel, latent=latent),
        grid=(nblk,), in_specs=in_specs, out_specs=out_specs, out_shape=out_shape,
        compiler_params=_params(1), name="inproj_lat" if latent else "inproj_ctx",
    )(*args)


def _cache_kernel(ckv_ref, kr_ref, sk_ref, sv_ref, wuk_ref, wuv_ref, km_o, vm_o, ks_o, vs_o):
    ckv = ckv_ref[0, 0].astype(BF16)
    kn = _dot(ckv, wuk_ref[...])
    kr = kr_ref[0, 0]
    for h in range(MLA_HEADS):
        sl = slice(h * LANES, (h + 1) * LANES)
        km_o[0, :, sl] = (kn[:, sl] + kr).astype(BF16)
    _store_vt(vm_o.at[0], wuv_ref[...], ckv)
    sk = sk_ref[0, 0]
    lane = lax.broadcasted_iota(jnp.int32, sk.shape, 1)
    lo = lane < SWA_HD
    ks_o[0, :, 0:LANES] = jnp.where(lo, sk, 0.0).astype(BF16)
    ks_o[0, :, LANES:2 * LANES] = jnp.where(lo, pltpu.roll(sk, SWA_HD, 1), 0.0).astype(BF16)
    vs_o[0] = _v_variants(sv_ref[0, 0]).astype(BF16)


def _cache_prep(layer, ckv, kr_pad, sk, sv, wp):
    b, _, p, _ = ckv.shape
    cin = pl.BlockSpec((1, 1, p, LANES), lambda i: (i, layer, 0, 0))
    out = lambda w: pl.BlockSpec((1, p, w), lambda i: (i, 0, 0))
    return pl.pallas_call(
        _cache_kernel, grid=(b,),
        in_specs=[cin, cin, cin, cin, _resident((MLA_KV_LORA, 4 * LANES)),
                  _resident((MLA_HEADS * MLA_V, MLA_KV_LORA))],
        out_specs=[out(512), pl.BlockSpec((1, MLA_HEADS * VT_ROWS, p), lambda i: (i, 0, 0)), out(256), out(512)],
        out_shape=[jax.ShapeDtypeStruct((b, p, 512), BF16),
                   jax.ShapeDtypeStruct((b, MLA_HEADS * VT_ROWS, p), BF16),
                   jax.ShapeDtypeStruct((b, p, 256), BF16), jax.ShapeDtypeStruct((b, p, 512), BF16)],
        compiler_params=_params(1), name="cache_prep",
    )(ckv, kr_pad, sk, sv, wp["wuk"], wp["wuv"])


PAIR = 2 * DN_CHUNK


def _bmm(a, b):
    return jnp.einsum("bij,bjk->bik", a, b, preferred_element_type=F32)


def _bmm_nt(a, b):
    return jnp.einsum("bik,bjk->bij", a, b, preferred_element_type=F32)


def _split(a):
    hi = a.astype(BF16)
    return hi, (a - hi.astype(F32)).astype(BF16)


def _mm_split(a2, b):
    c = a2.shape[1]
    r = _bmm(jnp.concatenate(_split(a2), axis=1), jnp.concatenate(_split(b), axis=1))
    return r[:, :c] + r[:, c:]


def _tri_inverse_dup(low2):
    c, n2 = low2.shape[1], low2.shape[2]
    eye2 = (lax.broadcasted_iota(jnp.int32, (c, n2), 0)
            == (lax.broadcasted_iota(jnp.int32, (c, n2), 1) & (c - 1))).astype(F32)
    pp = -low2
    tt = eye2 + pp
    plain = lambda a2, b: _bmm(a2[:, :, :c].astype(BF16), b.astype(BF16))
    pp = _mm_split(pp, pp)
    s = 2
    while 2 * s < c:
        mm = _mm_split if 2 * s < c // 2 else plain
        r = mm(pp, jnp.concatenate([pp, tt], axis=2))
        pp = r[:, :, :n2]
        tt = tt + r[:, :, n2:]
        s *= 2
    return tt + plain(pp, tt)


def _chunk_cumsum(x, forward):
    rows = x.shape[0]
    rc = lax.broadcasted_iota(jnp.int32, x.shape, 0) & (DN_CHUNK - 1)
    s = 1
    while s < DN_CHUNK:
        if forward:
            x = x + jnp.where(rc >= s, pltpu.roll(x, s, 0), 0.0)
        else:
            x = x + jnp.where(rc < DN_CHUNK - s, pltpu.roll(x, rows - s, 0), 0.0)
        s *= 2
    return x


def _dn_prep_kernel(qkv_ref, prev_ref, next_ref, small_ref, cw_ref, alog_ref, dtb_ref,
                    u_o, w_o, qg_o, kdt_o, intra_o, eg_o, *, tt, tiles_per_seq):
    C = DN_CHUNK
    i = pl.program_id(0)
    first = (i % tiles_per_seq) == 0
    last = (i % tiles_per_seq) == tiles_per_seq - 1
    row = lax.broadcasted_iota(jnp.int32, (tt, LANES), 0)
    n_chunks = tt // C
    hw = DN_HEADS * LANES

    def conv(c0):
        sl = slice(c0, c0 + LANES)
        x = qkv_ref[:, sl]
        prev_row = jnp.where(first, 0.0, prev_ref[7:8, sl])
        next_row = jnp.where(last, 0.0, next_ref[0:1, sl])
        xp = jnp.where(row == 0, prev_row, pltpu.roll(x, 1, 0))
        xn = jnp.where(row == tt - 1, next_row, pltpu.roll(x, tt - 1, 0))
        y = xp * cw_ref[0:1, sl] + x * cw_ref[1:2, sl] + xn * cw_ref[2:3, sl]
        return y * _sigmoid(y)

    small = small_ref[...]
    ri = lax.broadcasted_iota(jnp.int32, (C, LANES), 0)
    cj = lax.broadcasted_iota(jnp.int32, (C, LANES), 1)
    cm = cj & (C - 1)
    incl = (ri >= cm, ri <= cm)
    strict = (ri > cm, ri < cm)
    half = (cj < C, cj >= C)

    qs, ks, vs = [], [], []
    for j in range(DN_HEADS):
        q = conv(j * LANES)
        k = conv(hw + j * LANES)
        vs.append(conv(2 * hw + j * LANES))
        qs.append(q * lax.rsqrt(jnp.sum(q * q, -1, keepdims=True) + EPS) * (DN_DK ** -0.5))
        ks.append(k * lax.rsqrt(jnp.sum(k * k, -1, keepdims=True) + EPS))

    lane_t = lax.broadcasted_iota(jnp.int32, (tt, LANES), 1)
    xa = small + dtb_ref[...]
    softplus = jnp.maximum(xa, 0.0) + jnp.log(1.0 + jnp.exp(-jnp.abs(xa)))
    g_all = -jnp.exp(alog_ref[...]) * softplus
    beta_all = _sigmoid(small)
    gc_all = jnp.where(lane_t < DN_HEADS, _chunk_cumsum(g_all, True), _chunk_cumsum(g_all, False))
    gc_rows = gc_all.T
    left = lax.broadcasted_iota(jnp.int32, (1, LANES), 1) < C

    ch = lambda a, c: a[c * C:(c + 1) * C]
    hc = [(j, c) for j in range(DN_HEADS) for c in range(n_chunks)]
    kq = jnp.stack([jnp.concatenate([ch(ks[j], c), ch(qs[j], c)], axis=0).astype(BF16) for j, c in hc])
    kk = jnp.stack([jnp.concatenate([ch(ks[j], c)] * 2, axis=0).astype(BF16) for j, c in hc])
    gq = _bmm_nt(kq, kk)

    chains = [(j, c, d) for j, c in hc for d in range(2)]
    lows, rhss, decays, gcs = [], [], [], []
    for j, c, d in chains:
        b = hc.index((j, c))
        ln = DN_HEADS * d + j
        gc = jnp.broadcast_to(ch(gc_all, c)[:, ln:ln + 1], (C, LANES))
        beta = ch(beta_all, c)[:, 2 * DN_HEADS + ln:2 * DN_HEADS + ln + 1]
        pair_row = gc_rows[ln:ln + 1, (c // 2) * PAIR:(c // 2 + 1) * PAIR]
        swapped = pltpu.roll(pair_row, C, 1)
        key_row = jnp.where(left, pair_row, swapped) if c % 2 == 0 else jnp.where(left, swapped, pair_row)
        decay = jnp.exp(jnp.where(incl[d], gc - key_row, NEG_BIG))
        lows.append(jnp.where(strict[d], gq[b, :C] * beta * decay, 0.0))
        kc = ch(ks[j], c)
        rhss.append(jnp.concatenate([ch(vs[j], c) * beta, kc * beta * jnp.exp(gc)], axis=1).astype(BF16))
        decays.append(decay)
        gcs.append(gc)
    tinv = _tri_inverse_dup(jnp.stack(lows))
    th, tl = _split(tinv[:, :, :C])
    sol = _bmm(jnp.concatenate([th, tl], axis=1), jnp.stack(rhss))
    sol = sol[:, :C] + sol[:, C:]

    kdec = {}
    for n, (j, c, d) in enumerate(chains):
        b = hc.index((j, c))
        rows = slice(c * C, (c + 1) * C)
        cols = slice(j * LANES, (j + 1) * LANES)
        gc = gcs[n]
        u_o[d, rows, cols] = sol[n, :, :DN_DV].astype(BF16)
        w_o[d, rows, cols] = sol[n, :, DN_DV:].astype(BF16)
        intra_o[d, rows, cols] = jnp.where(incl[d] & half[c % 2], gq[b, C:] * decays[n], 0.0).astype(BF16)
        qg_o[d, rows, cols] = (ch(qs[j], c) * jnp.exp(gc)).astype(BF16)
        g_last = gc[C - 1:C, :] if d == 0 else gc[0:1, :]
        kdec[(j, c, d)] = ch(ks[j], c) * jnp.exp(g_last - gc)
        eg_o[d, c * 8:(c + 1) * 8, cols] = jnp.broadcast_to(jnp.exp(g_last), (8, LANES))
    for j in range(DN_HEADS):
        for m in range(n_chunks // 2):
            for d in range(2):
                slab = jnp.concatenate([kdec[(j, 2 * m, d)], kdec[(j, 2 * m + 1, d)]], axis=0)
                kdt_o[d, m * PAIR:(m + 1) * PAIR, j * LANES:(j + 1) * LANES] = slab.T.astype(BF16)


def _dn_prep(qkv, small, conv_w, a_log, dt_bias, *, seq):
    n = qkv.shape[0]
    tt = 256
    t8 = tt // 8
    last8 = n // 8 - 1
    hw = DN_HEADS * LANES
    lane_vec = lambda a: jnp.pad(a.reshape(1, 2 * DN_HEADS), ((0, 0), (0, LANES - 2 * DN_HEADS)))
    out = lambda rows: pl.BlockSpec((2, rows, hw), lambda i: (0, i, 0))
    big = lambda dt: jax.ShapeDtypeStruct((2, n, hw), dt)
    return pl.pallas_call(
        functools.partial(_dn_prep_kernel, tt=tt, tiles_per_seq=seq // tt),
        grid=(n // tt,),
        in_specs=[pl.BlockSpec((tt, 3 * hw), lambda i: (i, 0)),
                  pl.BlockSpec((8, 3 * hw), lambda i: (jnp.maximum(i * t8 - 1, 0), 0)),
                  pl.BlockSpec((8, 3 * hw), lambda i: (jnp.minimum((i + 1) * t8, last8), 0)),
                  pl.BlockSpec((tt, LANES), lambda i: (i, 0)),
                  _resident((3, 3 * hw)), _resident((1, LANES)), _resident((1, LANES))],
        out_specs=[out(tt)] * 5 + [out(t8)],
        out_shape=[big(BF16), big(BF16), big(BF16), big(BF16), big(BF16),
                   jax.ShapeDtypeStruct((2, n // 8, hw), F32)],
        compiler_params=_params(1), name="dn_prep",
    )(qkv, qkv, qkv, small, conv_w, lane_vec(a_log), lane_vec(dt_bias))


def _dn_scan_kernel(*refs, ts, bb, has_s0, want_state):
    it = iter(refs)
    ops = [[next(it) for _ in range(6)] for _ in range(2)]
    s0_ref = next(it) if has_s0 else None
    o_refs = [next(it), next(it)]
    sfin_ref = next(it) if want_state else None
    st = next(it)
    i = pl.program_id(1)
    chains = [(g, d, h) for g in range(bb) for d in range(2) for h in range(DN_HEADS)]

    @pl.when(i == 0)
    def _():
        for n, (g, d, h) in enumerate(chains):
            st[n] = s0_ref[g, d, h] if has_s0 else jnp.zeros((DN_DK, DN_DV), F32)

    n_chunks = ts // DN_CHUNK
    zeros = jnp.zeros((DN_CHUNK, DN_DV), F32)
    S = st[...]
    for step in range(n_chunks):
        chunk = lambda d: step if d == 0 else n_chunks - 1 - step
        wq, us, intras, kdts, egs = [], [], [], [], []
        for g, d, h in chains:
            u_ref, w_ref, qg_ref, kdt_ref, intra_ref, eg_ref = ops[d]
            c = chunk(d)
            rows = slice(c * DN_CHUNK, (c + 1) * DN_CHUNK)
            prow = slice((c // 2) * PAIR, (c // 2 + 1) * PAIR)
            cols = slice(h * LANES, (h + 1) * LANES)
            wq.append(jnp.concatenate([w_ref[0, g, rows, cols], qg_ref[0, g, rows, cols]], axis=0))
            us.append(u_ref[0, g, rows, cols])
            intras.append(intra_ref[0, g, rows, cols])
            kdts.append(kdt_ref[0, g, prow, cols])
            egs.append(eg_ref[0, g, c * 8:c * 8 + 1, cols])
        r = _bmm(jnp.stack(wq), S.astype(BF16))
        v_new = jnp.stack(us) - r[:, :DN_CHUNK]
        v_ext = jnp.stack([jnp.concatenate([v_new[n], zeros] if chunk(d) % 2 == 0 else [zeros, v_new[n]], axis=0)
                           for n, (g, d, h) in enumerate(chains)]).astype(BF16)
        o = r[:, DN_CHUNK:] + _bmm(jnp.stack(intras), v_ext)
        S = S * jnp.stack(egs) + _bmm(jnp.stack(kdts), v_ext)
        for n, (g, d, h) in enumerate(chains):
            c = chunk(d)
            o_refs[d][g, c * DN_CHUNK:(c + 1) * DN_CHUNK, h * LANES:(h + 1) * LANES] = o[n].astype(BF16)
    st[...] = S

    if want_state:
        @pl.when(i == pl.num_programs(1) - 1)
        def _():
            for n, (g, d, h) in enumerate(chains):
                sfin_ref[g, d, h] = S[n]


def _dn_scan(prep, s0, *, batch, seq, want_state):
    ts = min(seq, 512)
    nt = seq // ts
    bb = 2
    hw = DN_HEADS * LANES
    in_specs, args = [], []
    for d in range(2):
        imap = (lambda b, i: (0, b, i, 0)) if d == 0 else (lambda b, i: (1, b, nt - 1 - i, 0))
        for a in prep:
            rows = a.shape[1] // batch
            in_specs.append(pl.BlockSpec((1, bb, rows // nt, hw), imap))
            args.append(a.reshape(2, batch, rows, hw))
    state_spec = pl.BlockSpec((bb, 2, DN_HEADS, DN_DK, DN_DV), lambda b, i: (b, 0, 0, 0, 0))
    if s0 is not None:
        in_specs.append(state_spec)
        args.append(s0)
    out_specs = [pl.BlockSpec((bb, ts, hw), lambda b, i: (b, i, 0)),
                 pl.BlockSpec((bb, ts, hw), lambda b, i: (b, nt - 1 - i, 0))]
    out_shape = [jax.ShapeDtypeStruct((batch, seq, hw), BF16)] * 2
    if want_state:
        out_specs.append(state_spec)
        out_shape.append(jax.ShapeDtypeStruct((batch, 2, DN_HEADS, DN_DK, DN_DV), F32))
    res = pl.pallas_call(
        functools.partial(_dn_scan_kernel, ts=ts, bb=bb, has_s0=s0 is not None, want_state=want_state),
        grid=(batch // bb, nt), in_specs=in_specs, out_specs=out_specs, out_shape=out_shape,
        scratch_shapes=[pltpu.VMEM((2 * bb * DN_HEADS, DN_DK, DN_DV), F32)],
        compiler_params=_params(2), name="dn_scan",
    )(*args)
    return (res[0].reshape(batch * seq, hw), res[1].reshape(batch * seq, hw), res[2] if want_state else None)


def _swa_kernel(*refs, tq, seq, windowed, ctx_len):
    it = iter(refs)
    q_ref, k_ref, v_ref = next(it), next(it), next(it)
    kc_ref, vc_ref = (next(it), next(it)) if ctx_len else (None, None)
    sink_ref = next(it)
    o_ref = next(it)
    group = SWA_HEADS // SWA_KV_HEADS
    q0 = pl.program_id(1) * tq
    if windowed:
        wl = tq + 2 * WINDOW
        ws = pl.multiple_of(jnp.clip(q0 - WINDOW, 0, seq - wl), WINDOW)
        qpos = q0 + lax.broadcasted_iota(jnp.int32, (tq, wl), 0)
        kpos = ws + lax.broadcasted_iota(jnp.int32, (tq, wl), 1)
        valid = jnp.abs(qpos - kpos) <= WINDOW
        rows = pl.ds(ws, wl)
    else:
        rows = slice(None)
    def scores(h):
        kh = h // group
        qh = q_ref[0, :, h * LANES:(h + 1) * LANES]
        s = _dot_nt(qh, k_ref[0, rows, kh * LANES:(kh + 1) * LANES].astype(BF16))
        if windowed:
            s = jnp.where(valid, s, NEG_BIG)
        sc = _dot_nt(qh, kc_ref[0, :, kh * LANES:(kh + 1) * LANES]) if ctx_len else None
        return h, s, sc

    def values(h, s, sc):
        sink = sink_ref[h] * LOG2E
        m = jnp.maximum(jnp.max(s, axis=-1, keepdims=True), sink)
        if ctx_len:
            m = jnp.maximum(m, jnp.max(sc, axis=-1, keepdims=True))
        p = jnp.exp2(s - m)
        l = jnp.sum(p, axis=-1, keepdims=True) + jnp.exp2(sink - m)
        pv = _dot(p.astype(BF16), v_ref[0, rows, h * LANES:(h + 1) * LANES])
        if ctx_len:
            pc = jnp.exp2(sc - m)
            l = l + jnp.sum(pc, axis=-1, keepdims=True)
            pv = pv + _dot(pc.astype(BF16), vc_ref[0, :, h * LANES:(h + 1) * LANES])
        return pv / l

    outs, pending = [], None
    for h in range(SWA_HEADS):
        cur = scores(h)
        if pending is not None:
            outs.append(values(*pending))
        pending = cur
    outs.append(values(*pending))
    for pair in range(2):
        o_ref[0, :, pair * LANES:(pair + 1) * LANES] = (outs[2 * pair] + outs[2 * pair + 1]).astype(o_ref.dtype)


def _swa_attention(q, k, v, kc, vc, sink, *, windowed, tq):
    b, seq, _ = q.shape
    ctx_len = 0 if kc is None else kc.shape[1]
    full = lambda a: pl.BlockSpec((1,) + a.shape[1:], lambda i, j: (i, 0, 0))
    in_specs = [pl.BlockSpec((1, tq, 4 * LANES), lambda i, j: (i, j, 0)), full(k), full(v)]
    args = [q, k, v]
    if ctx_len:
        in_specs += [full(kc), full(vc)]
        args += [kc, vc]
    in_specs.append(pl.BlockSpec(memory_space=pltpu.SMEM))
    args.append(sink)
    return pl.pallas_call(
        functools.partial(_swa_kernel, tq=tq, seq=seq, windowed=windowed, ctx_len=ctx_len),
        grid=(b, seq // tq), in_specs=in_specs,
        out_specs=pl.BlockSpec((1, tq, 2 * LANES), lambda i, j: (i, j, 0)),
        out_shape=jax.ShapeDtypeStruct((b, seq, 2 * LANES), BF16),
        compiler_params=_params(2), name="swa_attn",
    )(*args)


def _mla_kernel(*refs, has_ctx):
    if has_ctx:
        q_ref, k_ref, vt_ref, kc_ref, vtc_ref, o_ref = refs
    else:
        q_ref, k_ref, vt_ref, o_ref = refs
    seq = k_ref.shape[1]
    kb = min(seq, 512)
    blocks = [(lambda sl, s=s: k_ref[0, s:s + kb, sl], lambda vr, s=s: vt_ref[vr, s:s + kb])
              for s in range(0, seq, kb)]
    if has_ctx:
        blocks.append((lambda sl: kc_ref[0, :, sl], lambda vr: vtc_ref[0, vr, :]))

    def scores(h):
        sl = slice(h * LANES, (h + 1) * LANES)
        qh = q_ref[0, :, sl]
        sts = [_dot_nt(keys(sl), qh) for keys, _ in blocks]
        m = functools.reduce(jnp.maximum, [jnp.max(st, axis=0, keepdims=True) for st in sts])
        return h, sts, m

    def values(h, sts, m):
        vr = slice(h * VT_ROWS, (h + 1) * VT_ROWS)
        r = None
        for st, (_, vals) in zip(sts, blocks):
            part = _dot(vals(vr), jnp.exp2(st - m).astype(BF16))
            r = part if r is None else r + part
        return r[:MLA_V] / r[MLA_V:MLA_V + 1]

    outs, pending = [], None
    for h in range(MLA_HEADS):
        cur = scores(h)
        if pending is not None:
            outs.append(values(*pending))
        pending = cur
    outs.append(values(*pending))
    o_ref[0] = jnp.concatenate(outs, axis=0).T.astype(o_ref.dtype)


def _mla_attention(q, k, vt, kc, vtc, *, tq):
    b, seq, _ = q.shape
    in_specs = [pl.BlockSpec((1, tq, 4 * LANES), lambda i, j: (i, j, 0)),
                pl.BlockSpec((1, seq, 4 * LANES), lambda i, j: (i, 0, 0)),
                pl.BlockSpec((MLA_HEADS * VT_ROWS, seq), lambda i, j: (0, i))]
    args = [q, k, vt]
    if kc is not None:
        p = kc.shape[1]
        in_specs += [pl.BlockSpec((1, p, 4 * LANES), lambda i, j: (i, 0, 0)),
                     pl.BlockSpec((1, MLA_HEADS * VT_ROWS, p), lambda i, j: (i, 0, 0))]
        args += [kc, vtc]
    return pl.pallas_call(
        functools.partial(_mla_kernel, has_ctx=kc is not None),
        grid=(b, seq // tq), in_specs=in_specs,
        out_specs=pl.BlockSpec((1, tq, MLA_HEADS * MLA_V), lambda i, j: (i, j, 0)),
        out_shape=jax.ShapeDtypeStruct((b, seq, MLA_HEADS * MLA_V), BF16),
        compiler_params=_params(2), name="mla_attn",
    )(*args)


def _ffn_kernel(*refs, final):
    if final:
        (x_ref, of_ref, obw_ref, z_ref, ng_ref, ob_ref, oc_ref, mod_ref, n2_ref, wo_ref, w1_ref, w2_ref,
         fg_ref, o_ref) = refs
    else:
        (x_ref, of_ref, obw_ref, z_ref, ng_ref, ob_ref, oc_ref, mod_ref, n2_ref, wo_ref, w1_ref, w2_ref,
         o_ref) = refs
    g1 = mod_ref[0, 2:3, :]
    sh2 = mod_ref[0, 3:4, :]
    sc2 = mod_ref[0, 4:5, :]
    g2 = mod_ref[0, 5:6, :]
    mix = _dot(ob_ref[...], wo_ref[512:768, :]) + _dot(oc_ref[...], wo_ref[768:1024, :])
    for h in range(DN_HEADS):
        sl = slice(h * LANES, (h + 1) * LANES)
        z = z_ref[:, sl]
        o = of_ref[:, sl].astype(F32) + obw_ref[:, sl].astype(F32)
        oa = _rms(o, ng_ref[...]) * (z * _sigmoid(z))
        mix = mix + _dot(oa.astype(BF16), wo_ref[sl, :])
    x = x_ref[...] + g1 * mix
    h2 = (_rms(x, n2_ref[...]) * (1.0 + sc2) + sh2).astype(BF16)
    acc = None
    fc = 1024
    for c in range(D_FF // fc):
        a = jnp.maximum(_dot(h2, w1_ref[:, c * fc:(c + 1) * fc]), 0.0)
        part = _dot((a * a).astype(BF16), w2_ref[c * fc:(c + 1) * fc, :])
        acc = part if acc is None else acc + part
    x = x + g2 * acc
    if final:
        x = _rms(x, fg_ref[...])
    o_ref[...] = x


def _ffn(x2d, o_fwd, o_bwd, z, dn_g, ob, oc, mod_l, n2g, wp, final_g, *, latent, seq, tm):
    n = x2d.shape[0]
    per_seq = seq // tm if latent else 1
    rows = lambda w: pl.BlockSpec((tm, w), lambda i: (i, 0))
    if latent:
        mod_spec = pl.BlockSpec((1, N_MOD, D_MODEL), lambda i: (i // per_seq, 0, 0))
    else:
        mod_spec = pl.BlockSpec((1, N_MOD, D_MODEL), lambda i: (MOD_ROWS // 2, 0, 0))
    in_specs = [rows(D_MODEL), rows(512), rows(512), rows(512), _resident((1, DN_DV)), rows(256), rows(256),
                mod_spec, _resident((1, D_MODEL)),
                _resident((D_MODEL, D_MODEL)), _resident((D_MODEL, D_FF)), _resident((D_FF, D_MODEL))]
    args = [x2d, o_fwd, o_bwd, z, dn_g, ob, oc, mod_l, n2g, wp["w_out"], wp["w_ff1"], wp["w_ff2"]]
    if final_g is not None:
        in_specs.append(_resident((1, D_MODEL)))
        args.append(final_g)
    return pl.pallas_call(
        functools.partial(_ffn_kernel, final=final_g is not None),
        grid=(n // tm,), in_specs=in_specs, out_specs=rows(D_MODEL),
        out_shape=jax.ShapeDtypeStruct((n, D_MODEL), F32),
        compiler_params=_params(1), name="ffn",
    )(*args)


def _rope_tables(n_tok, rot_dim, lane0):
    rows = n_tok // GRID_W
    row = np.repeat(np.arange(rows, dtype=np.float32), GRID_W)
    colp = np.tile(np.arange(GRID_W, dtype=np.float32), rows)
    n_freq = rot_dim // 4
    inv_freq = (np.float32(ROPE_BASE) ** (-np.arange(n_freq, dtype=np.float32) / np.float32(n_freq))).astype(np.float32)
    ang = np.concatenate([row[:, None] * inv_freq, colp[:, None] * inv_freq], axis=-1).astype(np.float32)
    cos = np.cos(ang.astype(np.float64)).astype(np.float32)
    sin = np.sin(ang.astype(np.float64)).astype(np.float32)
    half = rot_dim // 2
    zeros = lambda w: np.zeros((n_tok, w), np.float32)
    tail = LANES - lane0 - rot_dim
    c = np.concatenate([np.ones((n_tok, lane0), np.float32), cos, cos, zeros(tail)], axis=1)
    s1 = np.concatenate([zeros(lane0 + half), sin, zeros(tail)], axis=1)
    s2 = np.concatenate([zeros(lane0), -sin, zeros(half + tail)], axis=1)
    return jnp.asarray(c), jnp.asarray(s1), jnp.asarray(s2)


def _pad_heads(w, n_heads, width):
    k = w.shape[0]
    w = w.reshape(k, n_heads, width)
    return jnp.pad(w, ((0, 0), (0, 0), (0, LANES - width))).reshape(k, n_heads * LANES)


def _pack_layer(l, w_in, mla_q_norm_g, mla_w_uq, mla_kv_norm_g, mla_w_ukv, w_out, w_ff1, w_ff2):
    w = w_in[l]
    offs = np.cumsum([0, 1536, 512, 4, 4, 4, 4, MLA_Q_LORA, MLA_KV_LORA, MLA_ROPE, 256, 128, 128])
    piece = lambda i: w[:, offs[i]:offs[i + 1]]
    k = w.shape[0]
    small = jnp.concatenate([piece(2), piece(3), piece(4), piece(5), jnp.zeros((k, KR_LANE - 16), F32),
                             piece(8), jnp.zeros((k, LANES - KR_LANE - MLA_ROPE), F32)], axis=1)
    w_packed = jnp.concatenate([piece(0), piece(1), small, piece(6), piece(7),
                                _pad_heads(piece(9), SWA_HEADS, SWA_HD),
                                _pad_heads(piece(10), SWA_KV_HEADS, SWA_HD), piece(11)], axis=1)
    ukv = mla_w_ukv[l].reshape(MLA_KV_LORA, MLA_HEADS, MLA_NOPE + MLA_V)
    wuk = jnp.pad(ukv[:, :, :MLA_NOPE], ((0, 0), (0, 0), (0, LANES - MLA_NOPE)))
    wuv_t = ukv[:, :, MLA_NOPE:].reshape(MLA_KV_LORA, MLA_HEADS * MLA_V).T
    return dict(
        w_in=w_packed.astype(BF16),
        gq=mla_q_norm_g[l][None, :], gkv=mla_kv_norm_g[l][None, :],
        wuq=_pad_heads(mla_w_uq[l], MLA_HEADS, MLA_NOPE + MLA_ROPE).astype(BF16),
        wuk=wuk.reshape(MLA_KV_LORA, MLA_HEADS * LANES).astype(BF16),
        wuv=wuv_t.astype(BF16),
        w_out=w_out[l].astype(BF16), w_ff1=w_ff1[l].astype(BF16), w_ff2=w_ff2[l].astype(BF16))


def _layer(x2d, mod_l, wp, lp, ctx, final_g, *, batch, seq, latent, tables):
    tm = 512
    outs = _inproj(x2d, mod_l, lp["norm1_g"], wp, tables, latent=latent, seq=seq, tm=tm)
    qkv, z, small, qm, km, vm, qs, ks, vs = outs[:9]
    r3 = lambda a: a.reshape(batch, seq, a.shape[-1])
    prep = _dn_prep(qkv, small, lp["dn_conv_w"], lp["dn_a_log"], lp["dn_dt_bias"], seq=seq)
    o_fwd, o_bwd, state = _dn_scan(prep, ctx["dn"] if latent else None, batch=batch, seq=seq,
                                   want_state=not latent)
    tq = 256
    if latent:
        o_b = _mla_attention(r3(qm), r3(km), vm, ctx["km"], ctx["vm"], tq=tq)
        o_c = _swa_attention(r3(qs), r3(ks), r3(vs), ctx["ks"], ctx["vs"], lp["swa_sink"],
                             windowed=True, tq=tq)
    else:
        o_b = _mla_attention(r3(qm), r3(km), vm, None, None, tq=tq)
        o_c = _swa_attention(r3(qs), r3(ks), r3(vs), None, None, lp["swa_sink"], windowed=False, tq=tq)
    n = batch * seq
    x_new = _ffn(x2d, o_fwd, o_bwd, z, lp["dn_norm_g"], o_b.reshape(n, -1), o_c.reshape(n, -1), mod_l,
                 lp["norm2_g"], wp, final_g, latent=latent, seq=seq, tm=tm)
    if latent:
        return x_new, None
    ckv_n, sv = outs[9], outs[10]
    k_c = ks.reshape(batch, seq, SWA_KV_HEADS, LANES)[..., :SWA_HD]
    new = (state, ckv_n.reshape(batch, seq, MLA_KV_LORA),
           small[:, KR_LANE:KR_LANE + MLA_ROPE].reshape(batch, seq, MLA_ROPE),
           k_c, sv.reshape(batch, seq, SWA_KV_HEADS, SWA_HD))
    return x_new, new


def kernel(x_prompt, x_sample, state_dn, cache_mla_ckv, cache_mla_krope, cache_swa_k, cache_swa_v,
           c, c_ctx, ada_w, ada_b, norm1_g, norm2_g, w_in, dn_conv_w, dn_a_log, dn_dt_bias, dn_norm_g,
           mla_q_norm_g, mla_w_uq, mla_kv_norm_g, mla_w_ukv, swa_sink, w_out, w_ff1, w_ff2, final_norm_g):
    depth = ada_w.shape[0]
    bp, sp, _ = x_prompt.shape
    bs, ss, _ = x_sample.shape
    p_len = cache_mla_ckv.shape[2]
    assert bs <= MOD_ROWS // 2

    half_rows = MOD_ROWS // 2
    cvec = jnp.concatenate([c, jnp.zeros((half_rows - bs, D_MODEL), F32), c_ctx[None, :],
                            jnp.zeros((half_rows - 1, D_MODEL), F32)], axis=0)
    mod = _modulation(cvec, ada_w, ada_b).reshape(depth, MOD_ROWS, N_MOD, D_MODEL)

    packed = [_pack_layer(l, w_in, mla_q_norm_g, mla_w_uq, mla_kv_norm_g, mla_w_ukv, w_out, w_ff1, w_ff2)
              for l in range(depth)]
    lps = [dict(norm1_g=norm1_g[l][None, :], norm2_g=norm2_g[l][None, :], dn_conv_w=dn_conv_w[l],
                dn_a_log=dn_a_log[l], dn_dt_bias=dn_dt_bias[l], dn_norm_g=dn_norm_g[l][None, :],
                swa_sink=swa_sink[l]) for l in range(depth)]
    final_g = final_norm_g[None, :]

    xp = x_prompt.reshape(bp * sp, D_MODEL)
    news = []
    for l in range(depth):
        xp, new = _layer(xp, mod[l], packed[l], lps[l], None, final_g if l == depth - 1 else None,
                         batch=bp, seq=sp, latent=False, tables=None)
        news.append(new)
    y_prompt = xp.reshape(bp, sp, D_MODEL)

    tables = _rope_tables(ss, MLA_ROPE, KR_LANE) + _rope_tables(ss, SWA_HD, 0)
    kr_pad = jnp.pad(cache_mla_krope, ((0, 0), (0, 0), (0, 0), (KR_LANE, LANES - KR_LANE - MLA_ROPE)))
    sk_c = cache_swa_k.reshape(bs, depth, p_len, SWA_KV_HEADS * SWA_HD)
    sv_c = cache_swa_v.reshape(bs, depth, p_len, SWA_KV_HEADS * SWA_HD)
    xs = x_sample.reshape(bs * ss, D_MODEL)
    for l in range(depth):
        km, vm, ks, vs = _cache_prep(l, cache_mla_ckv, kr_pad, sk_c, sv_c, packed[l])
        ctx = dict(dn=state_dn[:, l], km=km, vm=vm, ks=ks, vs=vs)
        xs, _ = _layer(xs, mod[l], packed[l], lps[l], ctx, final_g if l == depth - 1 else None,
                       batch=bs, seq=ss, latent=True, tables=tables)
    y_sample = xs.reshape(bs, ss, D_MODEL)

    stack = lambda i: jnp.stack([nw[i] for nw in news], axis=1)
    return (y_prompt, y_sample, stack(0), stack(1), stack(2), stack(3), stack(4))
```

```python
import functools

import numpy as np
import jax
import jax.numpy as jnp
from jax import lax
from jax.experimental import pallas as pl
from jax.experimental.pallas import tpu as pltpu

F32 = jnp.float32
BF16 = jnp.bfloat16

D_MODEL = 1024
GRID_W = 64
EPS = 1e-6
ROPE_BASE = 10000.0
DN_HEADS = 4
DN_DK = 128
DN_DV = 128
DN_CHUNK = 64
MLA_HEADS = 4
MLA_Q_LORA = 256
MLA_KV_LORA = 128
MLA_NOPE = 64
MLA_ROPE = 32
MLA_V = 64
SWA_HEADS = 4
SWA_KV_HEADS = 2
SWA_HD = 64
WINDOW = 128
D_FF = 4 * D_MODEL
N_MOD = 6
LOG2E = 1.4426950408889634
MLA_SCALE = (MLA_NOPE + MLA_ROPE) ** -0.5
SWA_SCALE = SWA_HD ** -0.5

LANES = 128
MOD_ROWS = 16
NEG_BIG = -1e30
VMEM_LIMIT = 56 * 1024 * 1024

C_QKV = (0, 1536)
C_Z = (1536, 2048)
C_SMALL = (2048, 2176)
C_CQ = (2176, 2432)
C_CKV = (2432, 2560)
C_SQ = (2560, 3072)
C_SK = (3072, 3328)
C_SV = (3328, 3456)
IN_PACKED = 3456
KR_LANE = MLA_NOPE
VT_ROWS = MLA_V + 16


def _params(n_grid):
    return pltpu.CompilerParams(dimension_semantics=("arbitrary",) * n_grid,
                                vmem_limit_bytes=VMEM_LIMIT)


def _resident(shape):
    nd = len(shape)
    return pl.BlockSpec(shape, lambda *_: (0,) * nd, pipeline_mode=pl.Buffered(1))


def _dot(a, b):
    return jnp.dot(a, b, preferred_element_type=F32)


def _dot_nt(a, b):
    return lax.dot_general(a, b, (((1,), (1,)), ((), ())), preferred_element_type=F32)


def _rms(x, g):
    return x * lax.rsqrt(jnp.mean(x * x, axis=-1, keepdims=True) + EPS) * g


def _sigmoid(x):
    return 1.0 / (1.0 + jnp.exp(-x))


def _rope(x, c, s1, s2, half):
    return x * c + pltpu.roll(x, half, 1) * s1 + pltpu.roll(x, LANES - half, 1) * s2


def _mod_kernel(c_ref, w_ref, b_ref, o_ref):
    c = c_ref[...]
    s = (c * _sigmoid(c)).astype(BF16)
    o_ref[0] = _dot(s, w_ref[0].astype(BF16)) + b_ref[0]


def _modulation(cvec, ada_w, ada_b):
    L = ada_w.shape[0]
    nb = 768
    return pl.pallas_call(
        _mod_kernel,
        grid=(L, N_MOD * D_MODEL // nb),
        in_specs=[pl.BlockSpec((MOD_ROWS, D_MODEL), lambda l, j: (0, 0)),
                  pl.BlockSpec((1, D_MODEL, nb), lambda l, j: (l, 0, j)),
                  pl.BlockSpec((1, 1, nb), lambda l, j: (l, 0, j))],
        out_specs=pl.BlockSpec((1, MOD_ROWS, nb), lambda l, j: (l, 0, j)),
        out_shape=jax.ShapeDtypeStruct((L, MOD_ROWS, N_MOD * D_MODEL), F32),
        compiler_params=_params(2),
        name="modulation",
    )(cvec, ada_w, ada_b.reshape(L, 1, N_MOD * D_MODEL))


def _v_variants(v):
    lane = lax.broadcasted_iota(jnp.int32, v.shape, 1)
    lo = lane < SWA_HD
    sw = pltpu.roll(v, SWA_HD, 1)
    zero = jnp.zeros_like(v)
    return jnp.concatenate([jnp.where(lo, v, zero), jnp.where(lo, zero, sw),
                            jnp.where(lo, sw, zero), jnp.where(lo, zero, v)], axis=1)


def _store_vt(vt_ref, wuv_t, ckv):
    vt = _dot_nt(wuv_t, ckv)
    ones = jnp.ones((VT_ROWS - MLA_V, vt.shape[1]), BF16)
    for h in range(MLA_HEADS):
        vt_ref[h * VT_ROWS:h * VT_ROWS + MLA_V, :] = vt[h * MLA_V:(h + 1) * MLA_V].astype(BF16)
        vt_ref[h * VT_ROWS + MLA_V:(h + 1) * VT_ROWS, :] = ones


def _inproj_kernel(*refs, latent):
    if latent:
        (x_ref, mod_ref, g1_ref, w_ref, gq_ref, wuq_ref, gkv_ref, wuk_ref, wuv_ref,
         mc_ref, ms1_ref, ms2_ref, sc_ref, ss1_ref, ss2_ref,
         qkv_o, z_o, small_o, qm_o, km_o, vm_o, qs_o, ks_o, vs_o) = refs
    else:
        (x_ref, mod_ref, g1_ref, w_ref, gq_ref, wuq_ref, gkv_ref, wuk_ref, wuv_ref,
         qkv_o, z_o, small_o, qm_o, km_o, vm_o, qs_o, ks_o, vs_o, ckv_o, sv_o) = refs
    x = x_ref[...]
    sh1 = mod_ref[0, 0:1, :]
    sc1 = mod_ref[0, 1:2, :]
    hb = (_rms(x, g1_ref[...]) * (1.0 + sc1) + sh1).astype(BF16)

    def proj(cols):
        return _dot(hb, w_ref[:, cols[0]:cols[1]])

    small = proj(C_SMALL)
    small_o[...] = small
    cq = proj(C_CQ)
    ckv = proj(C_CKV)
    sq = proj(C_SQ)
    sk = proj(C_SK)
    sv = proj(C_SV)

    cqn = _rms(cq, gq_ref[...]).astype(BF16)
    q = _dot(cqn, wuq_ref[...])
    ckv_n = _rms(ckv, gkv_ref[...])
    ckv_b = ckv_n.astype(BF16)
    kn = _dot(ckv_b, wuk_ref[...])
    _store_vt(vm_o, wuv_ref[...], ckv_b)
    qkv_o[...] = proj(C_QKV)
    z_o[...] = proj(C_Z)
    lane = lax.broadcasted_iota(jnp.int32, small.shape, 1)
    kr = jnp.where((lane >= KR_LANE) & (lane < KR_LANE + MLA_ROPE), small, 0.0)
    if latent:
        mc, ms1, ms2 = mc_ref[...], ms1_ref[...], ms2_ref[...]
        kr = _rope(kr, mc, ms1, ms2, MLA_ROPE // 2)
    for h in range(MLA_HEADS):
        sl = slice(h * LANES, (h + 1) * LANES)
        qh = q[:, sl]
        if latent:
            qh = _rope(qh, mc, ms1, ms2, MLA_ROPE // 2)
        qm_o[:, sl] = (qh * (MLA_SCALE * LOG2E)).astype(BF16)
        km_o[:, sl] = (kn[:, sl] + kr).astype(BF16)

    if latent:
        sc, ss1, ss2 = sc_ref[...], ss1_ref[...], ss2_ref[...]
    for h in range(SWA_HEADS):
        sl = slice(h * LANES, (h + 1) * LANES)
        qh = sq[:, sl]
        if latent:
            qh = _rope(qh, sc, ss1, ss2, SWA_HD // 2)
        qs_o[:, sl] = (qh * (SWA_SCALE * LOG2E)).astype(BF16)
    for j in range(SWA_KV_HEADS):
        sl = slice(j * LANES, (j + 1) * LANES)
        kh = sk[:, sl]
        if latent:
            kh = _rope(kh, sc, ss1, ss2, SWA_HD // 2)
        ks_o[:, sl] = kh.astype(ks_o.dtype)
    vs_o[...] = _v_variants(sv).astype(BF16)
    if not latent:
        ckv_o[...] = ckv_n
        sv_o[...] = sv


def _inproj(x2d, mod_l, g1, wp, tables, *, latent, seq, tm):
    n = x2d.shape[0]
    nblk = n // tm
    per_seq = seq // tm if latent else 1
    rows = lambda w: pl.BlockSpec((tm, w), lambda i: (i, 0))
    if latent:
        mod_spec = pl.BlockSpec((1, N_MOD, D_MODEL), lambda i: (i // per_seq, 0, 0))
    else:
        mod_spec = pl.BlockSpec((1, N_MOD, D_MODEL), lambda i: (MOD_ROWS // 2, 0, 0))
    in_specs = [rows(D_MODEL), mod_spec, _resident((1, D_MODEL)), _resident((D_MODEL, IN_PACKED)),
                _resident((1, MLA_Q_LORA)), _resident((MLA_Q_LORA, 4 * LANES)),
                _resident((1, MLA_KV_LORA)), _resident((MLA_KV_LORA, 4 * LANES)),
                _resident((MLA_HEADS * MLA_V, MLA_KV_LORA))]
    args = [x2d, mod_l, g1, wp["w_in"], wp["gq"], wp["wuq"], wp["gkv"], wp["wuk"], wp["wuv"]]
    out_shape = [jax.ShapeDtypeStruct((n, 1536), F32), jax.ShapeDtypeStruct((n, 512), F32),
                 jax.ShapeDtypeStruct((n, LANES), F32),
                 jax.ShapeDtypeStruct((n, 512), BF16), jax.ShapeDtypeStruct((n, 512), BF16),
                 jax.ShapeDtypeStruct((MLA_HEADS * VT_ROWS, n), BF16),
                 jax.ShapeDtypeStruct((n, 512), BF16),
                 jax.ShapeDtypeStruct((n, 256), BF16 if latent else F32),
                 jax.ShapeDtypeStruct((n, 512), BF16)]
    out_specs = [rows(1536), rows(512), rows(LANES), rows(512), rows(512),
                 pl.BlockSpec((MLA_HEADS * VT_ROWS, tm), lambda i: (0, i)),
                 rows(512), rows(256), rows(512)]
    if latent:
        tab = pl.BlockSpec((tm, LANES), lambda i: (i % per_seq, 0))
        in_specs += [tab] * 6
        args += list(tables)
    else:
        out_shape += [jax.ShapeDtypeStruct((n, LANES), F32), jax.ShapeDtypeStruct((n, LANES), F32)]
        out_specs += [rows(LANES), rows(LANES)]
    return pl.pallas_call(
        functools.partial(_inproj_kernel, latent=latent),
        grid=(nblk,), in_specs=in_specs, out_specs=out_specs, out_shape=out_shape,
        compiler_params=_params(1), name="inproj_lat" if latent else "inproj_ctx",
    )(*args)


def _cache_kernel(ckv_ref, kr_ref, sk_ref, sv_ref, wuk_ref, wuv_ref, km_o, vm_o, ks_o, vs_o):
    ckv = ckv_ref[0, 0].astype(BF16)
    kn = _dot(ckv, wuk_ref[...])
    kr = kr_ref[0, 0]
    for h in range(MLA_HEADS):
        sl = slice(h * LANES, (h + 1) * LANES)
        km_o[0, :, sl] = (kn[:, sl] + kr).astype(BF16)
    _store_vt(vm_o.at[0], wuv_ref[...], ckv)
    sk = sk_ref[0, 0]
    lane = lax.broadcasted_iota(jnp.int32, sk.shape, 1)
    lo = lane < SWA_HD
    ks_o[0, :, 0:LANES] = jnp.where(lo, sk, 0.0).astype(BF16)
    ks_o[0, :, LANES:2 * LANES] = jnp.where(lo, pltpu.roll(sk, SWA_HD, 1), 0.0).astype(BF16)
    vs_o[0] = _v_variants(sv_ref[0, 0]).astype(BF16)


def _cache_prep(layer, ckv, kr_pad, sk, sv, wp):
    b, _, p, _ = ckv.shape
    cin = pl.BlockSpec((1, 1, p, LANES), lambda i: (i, layer, 0, 0))
    out = lambda w: pl.BlockSpec((1, p, w), lambda i: (i, 0, 0))
    return pl.pallas_call(
        _cache_kernel, grid=(b,),
        in_specs=[cin, cin, cin, cin, _resident((MLA_KV_LORA, 4 * LANES)),
                  _resident((MLA_HEADS * MLA_V, MLA_KV_LORA))],
        out_specs=[out(512), pl.BlockSpec((1, MLA_HEADS * VT_ROWS, p), lambda i: (i, 0, 0)), out(256), out(512)],
        out_shape=[jax.ShapeDtypeStruct((b, p, 512), BF16),
                   jax.ShapeDtypeStruct((b, MLA_HEADS * VT_ROWS, p), BF16),
                   jax.ShapeDtypeStruct((b, p, 256), BF16), jax.ShapeDtypeStruct((b, p, 512), BF16)],
        compiler_params=_params(1), name="cache_prep",
    )(ckv, kr_pad, sk, sv, wp["wuk"], wp["wuv"])


PAIR = 2 * DN_CHUNK


def _bmm(a, b):
    return jnp.einsum("bij,bjk->bik", a, b, preferred_element_type=F32)


def _bmm_nt(a, b):
    return jnp.einsum("bik,bjk->bij", a, b, preferred_element_type=F32)


def _split(a):
    hi = a.astype(BF16)
    return hi, (a - hi.astype(F32)).astype(BF16)


def _block_diag(x):
    first = (lax.broadcasted_iota(jnp.int32, x.shape, 2) & (LANES - 1)) < DN_CHUNK
    zero = jnp.zeros_like(x)
    return jnp.concatenate([jnp.where(first, x, zero), jnp.where(first, zero, x)], axis=1)


def _mm_packed(a, b, split):
    if not split:
        return _bmm(a.astype(BF16), _block_diag(b.astype(BF16)))
    c = a.shape[1]
    ah, al = _split(a)
    bh, bl = _split(b)
    lhs = jnp.concatenate([jnp.concatenate([ah, ah], axis=2), jnp.concatenate([al, al], axis=2)], axis=1)
    r = _bmm(lhs, jnp.concatenate([_block_diag(bh), _block_diag(bl)], axis=1))
    return r[:, :c] + r[:, c:]


def _tri_inverse_packed(low):
    c, n = low.shape[1], low.shape[2]
    eye = (lax.broadcasted_iota(jnp.int32, (c, n), 0)
           == (lax.broadcasted_iota(jnp.int32, (c, n), 1) & (c - 1))).astype(F32)
    pp = -low
    tt = eye + pp
    pp = _mm_packed(pp, pp, True)
    s = 2
    while 2 * s < c:
        r = _mm_packed(pp, jnp.concatenate([pp, tt], axis=2), 2 * s < c // 2)
        pp = r[:, :, :n]
        tt = tt + r[:, :, n:]
        s *= 2
    return tt + _mm_packed(pp, tt, False)


def _chunk_cumsum(x, forward):
    rows = x.shape[0]
    rc = lax.broadcasted_iota(jnp.int32, x.shape, 0) & (DN_CHUNK - 1)
    s = 1
    while s < DN_CHUNK:
        if forward:
            x = x + jnp.where(rc >= s, pltpu.roll(x, s, 0), 0.0)
        else:
            x = x + jnp.where(rc < DN_CHUNK - s, pltpu.roll(x, rows - s, 0), 0.0)
        s *= 2
    return x


def _dn_prep_kernel(qkv_ref, prev_ref, next_ref, small_ref, cw_ref, alog_ref, dtb_ref,
                    u_o, w_o, qg_o, kdt_o, intra_o, eg_o, *, tt, tiles_per_seq):
    C = DN_CHUNK
    i = pl.program_id(0)
    first = (i % tiles_per_seq) == 0
    last = (i % tiles_per_seq) == tiles_per_seq - 1
    row = lax.broadcasted_iota(jnp.int32, (tt, LANES), 0)
    n_chunks = tt // C
    hw = DN_HEADS * LANES

    def conv(c0):
        sl = slice(c0, c0 + LANES)
        x = qkv_ref[:, sl]
        prev_row = jnp.where(first, 0.0, prev_ref[7:8, sl])
        next_row = jnp.where(last, 0.0, next_ref[0:1, sl])
        xp = jnp.where(row == 0, prev_row, pltpu.roll(x, 1, 0))
        xn = jnp.where(row == tt - 1, next_row, pltpu.roll(x, tt - 1, 0))
        y = xp * cw_ref[0:1, sl] + x * cw_ref[1:2, sl] + xn * cw_ref[2:3, sl]
        return y * _sigmoid(y)

    small = small_ref[...]
    ri = lax.broadcasted_iota(jnp.int32, (C, LANES), 0)
    cj = lax.broadcasted_iota(jnp.int32, (C, LANES), 1)
    cm = cj & (C - 1)
    incl = (ri >= cm, ri <= cm)
    strict = (ri > cm, ri < cm)
    half = (cj < C, cj >= C)

    qs, ks, vs = [], [], []
    for j in range(DN_HEADS):
        q = conv(j * LANES)
        k = conv(hw + j * LANES)
        vs.append(conv(2 * hw + j * LANES))
        qs.append(q * lax.rsqrt(jnp.sum(q * q, -1, keepdims=True) + EPS) * (DN_DK ** -0.5))
        ks.append(k * lax.rsqrt(jnp.sum(k * k, -1, keepdims=True) + EPS))

    lane_t = lax.broadcasted_iota(jnp.int32, (tt, LANES), 1)
    xa = small + dtb_ref[...]
    softplus = jnp.maximum(xa, 0.0) + jnp.log(1.0 + jnp.exp(-jnp.abs(xa)))
    g_all = -jnp.exp(alog_ref[...]) * softplus
    beta_all = _sigmoid(small)
    gc_all = jnp.where(lane_t < DN_HEADS, _chunk_cumsum(g_all, True), _chunk_cumsum(g_all, False))
    gc_rows = gc_all.T

    ch = lambda a, c: a[c * C:(c + 1) * C]
    pr = lambda a, m: a[m * PAIR:(m + 1) * PAIR]
    hp = [(j, m) for j in range(DN_HEADS) for m in range(n_chunks // 2)]
    kq = jnp.stack([jnp.concatenate([pr(ks[j], m), pr(qs[j], m)], axis=0).astype(BF16) for j, m in hp])
    kk = jnp.stack([pr(ks[j], m).astype(BF16) for j, m in hp])
    gq = _bmm_nt(kq, kk)

    chains = [(j, m, d) for j, m in hp for d in range(2)]
    lows, intras, sol_rhs = [], [], []
    gcb = {}
    for j, m, d in chains:
        b = hp.index((j, m))
        ln = DN_HEADS * d + j
        bl = 2 * DN_HEADS + ln
        cols = slice(j * LANES, (j + 1) * LANES)
        for c in (2 * m, 2 * m + 1):
            gcb[(j, c, d)] = jnp.broadcast_to(ch(gc_all, c)[:, ln:ln + 1], (C, LANES))
        gc_p = jnp.where(half[0], gcb[(j, 2 * m, d)], gcb[(j, 2 * m + 1, d)])
        beta_p = jnp.where(half[0], jnp.broadcast_to(ch(beta_all, 2 * m)[:, bl:bl + 1], (C, LANES)),
                           jnp.broadcast_to(ch(beta_all, 2 * m + 1)[:, bl:bl + 1], (C, LANES)))
        key_row = gc_rows[ln:ln + 1, m * PAIR:(m + 1) * PAIR]
        decay = jnp.exp(jnp.where(incl[d], gc_p - key_row, NEG_BIG))
        gram = jnp.where(half[0], gq[b, 0:C], gq[b, C:PAIR])
        qk = jnp.where(half[0], gq[b, PAIR:PAIR + C], gq[b, PAIR + C:])
        lows.append(jnp.where(strict[d], gram * beta_p * decay, 0.0))
        intra = jnp.where(incl[d], qk * decay, 0.0)
        intra_o[d, (2 * m) * C:(2 * m + 1) * C, cols] = jnp.where(half[0], intra, 0.0).astype(BF16)
        intra_o[d, (2 * m + 1) * C:(2 * m + 2) * C, cols] = jnp.where(half[0], 0.0, intra).astype(BF16)
        rhs = []
        for c in (2 * m, 2 * m + 1):
            beta = ch(beta_all, c)[:, bl:bl + 1]
            kc = ch(ks[j], c)
            rhs.append(jnp.concatenate([ch(vs[j], c) * beta, kc * beta * jnp.exp(gcb[(j, c, d)])], axis=1))
        sol_rhs.append(jnp.concatenate(rhs, axis=0).astype(BF16))
    tinv = _tri_inverse_packed(jnp.stack(lows))

    th, tl = _split(tinv)
    first_half = lax.broadcasted_iota(jnp.int32, th.shape, 2) < C
    zero = jnp.zeros_like(th)
    sol = _bmm(jnp.concatenate([jnp.where(first_half, th, zero), jnp.where(first_half, tl, zero),
                                jnp.where(first_half, zero, th), jnp.where(first_half, zero, tl)], axis=1),
               jnp.stack(sol_rhs))

    for n, (j, m, d) in enumerate(chains):
        cols = slice(j * LANES, (j + 1) * LANES)
        kdec = []
        for i, c in enumerate((2 * m, 2 * m + 1)):
            rows = slice(c * C, (c + 1) * C)
            uw = sol[n, 2 * i * C:(2 * i + 1) * C] + sol[n, (2 * i + 1) * C:(2 * i + 2) * C]
            u_o[d, rows, cols] = uw[:, :DN_DV].astype(BF16)
            w_o[d, rows, cols] = uw[:, DN_DV:].astype(BF16)
            gc = gcb[(j, c, d)]
            qg_o[d, rows, cols] = (ch(qs[j], c) * jnp.exp(gc)).astype(BF16)
            g_last = gc[C - 1:C, :] if d == 0 else gc[0:1, :]
            kdec.append(ch(ks[j], c) * jnp.exp(g_last - gc))
            eg_o[d, c * 8:(c + 1) * 8, cols] = jnp.broadcast_to(jnp.exp(g_last), (8, LANES))
        kdt_o[d, m * PAIR:(m + 1) * PAIR, cols] = jnp.concatenate(kdec, axis=0).T.astype(BF16)


def _dn_prep(qkv, small, conv_w, a_log, dt_bias, *, seq):
    n = qkv.shape[0]
    tt = 256
    t8 = tt // 8
    last8 = n // 8 - 1
    hw = DN_HEADS * LANES
    lane_vec = lambda a: jnp.pad(a.reshape(1, 2 * DN_HEADS), ((0, 0), (0, LANES - 2 * DN_HEADS)))
    out = lambda rows: pl.BlockSpec((2, rows, hw), lambda i: (0, i, 0))
    big = lambda dt: jax.ShapeDtypeStruct((2, n, hw), dt)
    return pl.pallas_call(
        functools.partial(_dn_prep_kernel, tt=tt, tiles_per_seq=seq // tt),
        grid=(n // tt,),
        in_specs=[pl.BlockSpec((tt, 3 * hw), lambda i: (i, 0)),
                  pl.BlockSpec((8, 3 * hw), lambda i: (jnp.maximum(i * t8 - 1, 0), 0)),
                  pl.BlockSpec((8, 3 * hw), lambda i: (jnp.minimum((i + 1) * t8, last8), 0)),
                  pl.BlockSpec((tt, LANES), lambda i: (i, 0)),
                  _resident((3, 3 * hw)), _resident((1, LANES)), _resident((1, LANES))],
        out_specs=[out(tt)] * 5 + [out(t8)],
        out_shape=[big(BF16), big(BF16), big(BF16), big(BF16), big(BF16),
                   jax.ShapeDtypeStruct((2, n // 8, hw), F32)],
        compiler_params=_params(1), name="dn_prep",
    )(qkv, qkv, qkv, small, conv_w, lane_vec(a_log), lane_vec(dt_bias))


def _dn_scan_kernel(*refs, ts, bb, has_s0, want_state):
    it = iter(refs)
    ops = [[next(it) for _ in range(6)] for _ in range(2)]
    s0_ref = next(it) if has_s0 else None
    o_refs = [next(it), next(it)]
    sfin_ref = next(it) if want_state else None
    st = next(it)
    i = pl.program_id(1)
    chains = [(g, d, h) for g in range(bb) for d in range(2) for h in range(DN_HEADS)]

    @pl.when(i == 0)
    def _():
        for n, (g, d, h) in enumerate(chains):
            st[n] = s0_ref[g, d, h] if has_s0 else jnp.zeros((DN_DK, DN_DV), F32)

    n_chunks = ts // DN_CHUNK
    zeros = jnp.zeros((DN_CHUNK, DN_DV), F32)
    S = st[...]
    for step in range(n_chunks):
        chunk = lambda d: step if d == 0 else n_chunks - 1 - step
        wq, us, intras, kdts, egs = [], [], [], [], []
        for g, d, h in chains:
            u_ref, w_ref, qg_ref, kdt_ref, intra_ref, eg_ref = ops[d]
            c = chunk(d)
            rows = slice(c * DN_CHUNK, (c + 1) * DN_CHUNK)
            prow = slice((c // 2) * PAIR, (c // 2 + 1) * PAIR)
            cols = slice(h * LANES, (h + 1) * LANES)
            wq.append(jnp.concatenate([w_ref[0, g, rows, cols], qg_ref[0, g, rows, cols]], axis=0))
            us.append(u_ref[0, g, rows, cols])
            intras.append(intra_ref[0, g, rows, cols])
            kdts.append(kdt_ref[0, g, prow, cols])
            egs.append(eg_ref[0, g, c * 8:c * 8 + 1, cols])
        r = _bmm(jnp.stack(wq), S.astype(BF16))
        v_new = jnp.stack(us) - r[:, :DN_CHUNK]
        v_ext = jnp.stack([jnp.concatenate([v_new[n], zeros] if chunk(d) % 2 == 0 else [zeros, v_new[n]], axis=0)
                           for n, (g, d, h) in enumerate(chains)]).astype(BF16)
        o = r[:, DN_CHUNK:] + _bmm(jnp.stack(intras), v_ext)
        S = S * jnp.stack(egs) + _bmm(jnp.stack(kdts), v_ext)
        for n, (g, d, h) in enumerate(chains):
            c = chunk(d)
            o_refs[d][g, c * DN_CHUNK:(c + 1) * DN_CHUNK, h * LANES:(h + 1) * LANES] = o[n].astype(BF16)
    st[...] = S

    if want_state:
        @pl.when(i == pl.num_programs(1) - 1)
        def _():
            for n, (g, d, h) in enumerate(chains):
                sfin_ref[g, d, h] = S[n]


def _dn_scan(prep, s0, *, batch, seq, want_state):
    ts = min(seq, 512)
    nt = seq // ts
    bb = 2
    hw = DN_HEADS * LANES
    in_specs, args = [], []
    for d in range(2):
        imap = (lambda b, i: (0, b, i, 0)) if d == 0 else (lambda b, i: (1, b, nt - 1 - i, 0))
        for a in prep:
            rows = a.shape[1] // batch
            in_specs.append(pl.BlockSpec((1, bb, rows // nt, hw), imap))
            args.append(a.reshape(2, batch, rows, hw))
    state_spec = pl.BlockSpec((bb, 2, DN_HEADS, DN_DK, DN_DV), lambda b, i: (b, 0, 0, 0, 0))
    if s0 is not None:
        in_specs.append(state_spec)
        args.append(s0)
    out_specs = [pl.BlockSpec((bb, ts, hw), lambda b, i: (b, i, 0)),
                 pl.BlockSpec((bb, ts, hw), lambda b, i: (b, nt - 1 - i, 0))]
    out_shape = [jax.ShapeDtypeStruct((batch, seq, hw), BF16)] * 2
    if want_state:
        out_specs.append(state_spec)
        out_shape.append(jax.ShapeDtypeStruct((batch, 2, DN_HEADS, DN_DK, DN_DV), F32))
    res = pl.pallas_call(
        functools.partial(_dn_scan_kernel, ts=ts, bb=bb, has_s0=s0 is not None, want_state=want_state),
        grid=(batch // bb, nt), in_specs=in_specs, out_specs=out_specs, out_shape=out_shape,
        scratch_shapes=[pltpu.VMEM((2 * bb * DN_HEADS, DN_DK, DN_DV), F32)],
        compiler_params=_params(2), name="dn_scan",
    )(*args)
    return (res[0].reshape(batch * seq, hw), res[1].reshape(batch * seq, hw), res[2] if want_state else None)


def _swa_kernel(*refs, tq, seq, windowed, ctx_len):
    it = iter(refs)
    q_ref, k_ref, v_ref = next(it), next(it), next(it)
    kc_ref, vc_ref = (next(it), next(it)) if ctx_len else (None, None)
    sink_ref = next(it)
    o_ref = next(it)
    group = SWA_HEADS // SWA_KV_HEADS
    q0 = pl.program_id(1) * tq
    if windowed:
        wl = tq + 2 * WINDOW
        ws = pl.multiple_of(jnp.clip(q0 - WINDOW, 0, seq - wl), WINDOW)
        qpos = q0 + lax.broadcasted_iota(jnp.int32, (tq, wl), 0)
        kpos = ws + lax.broadcasted_iota(jnp.int32, (tq, wl), 1)
        valid = jnp.abs(qpos - kpos) <= WINDOW
        rows = pl.ds(ws, wl)
    else:
        rows = slice(None)
    def scores(h):
        kh = h // group
        qh = q_ref[0, :, h * LANES:(h + 1) * LANES]
        s = _dot_nt(qh, k_ref[0, rows, kh * LANES:(kh + 1) * LANES].astype(BF16))
        if windowed:
            s = jnp.where(valid, s, NEG_BIG)
        sc = _dot_nt(qh, kc_ref[0, :, kh * LANES:(kh + 1) * LANES]) if ctx_len else None
        return h, s, sc

    def values(h, s, sc):
        sink = sink_ref[h] * LOG2E
        m = jnp.maximum(jnp.max(s, axis=-1, keepdims=True), sink)
        if ctx_len:
            m = jnp.maximum(m, jnp.max(sc, axis=-1, keepdims=True))
        p = jnp.exp2(s - m)
        l = jnp.sum(p, axis=-1, keepdims=True) + jnp.exp2(sink - m)
        pv = _dot(p.astype(BF16), v_ref[0, rows, h * LANES:(h + 1) * LANES])
        if ctx_len:
            pc = jnp.exp2(sc - m)
            l = l + jnp.sum(pc, axis=-1, keepdims=True)
            pv = pv + _dot(pc.astype(BF16), vc_ref[0, :, h * LANES:(h + 1) * LANES])
        return pv / l

    outs, pending = [], None
    for h in range(SWA_HEADS):
        cur = scores(h)
        if pending is not None:
            outs.append(values(*pending))
        pending = cur
    outs.append(values(*pending))
    for pair in range(2):
        o_ref[0, :, pair * LANES:(pair + 1) * LANES] = (outs[2 * pair] + outs[2 * pair + 1]).astype(o_ref.dtype)


def _swa_attention(q, k, v, kc, vc, sink, *, windowed, tq):
    b, seq, _ = q.shape
    ctx_len = 0 if kc is None else kc.shape[1]
    full = lambda a: pl.BlockSpec((1,) + a.shape[1:], lambda i, j: (i, 0, 0))
    in_specs = [pl.BlockSpec((1, tq, 4 * LANES), lambda i, j: (i, j, 0)), full(k), full(v)]
    args = [q, k, v]
    if ctx_len:
        in_specs += [full(kc), full(vc)]
        args += [kc, vc]
    in_specs.append(pl.BlockSpec(memory_space=pltpu.SMEM))
    args.append(sink)
    return pl.pallas_call(
        functools.partial(_swa_kernel, tq=tq, seq=seq, windowed=windowed, ctx_len=ctx_len),
        grid=(b, seq // tq), in_specs=in_specs,
        out_specs=pl.BlockSpec((1, tq, 2 * LANES), lambda i, j: (i, j, 0)),
        out_shape=jax.ShapeDtypeStruct((b, seq, 2 * LANES), BF16),
        compiler_params=_params(2), name="swa_attn",
    )(*args)


def _mla_kernel(*refs, has_ctx):
    if has_ctx:
        q_ref, k_ref, vt_ref, kc_ref, vtc_ref, o_ref = refs
    else:
        q_ref, k_ref, vt_ref, o_ref = refs
    seq = k_ref.shape[1]
    kb = min(seq, 512)
    blocks = [(lambda sl, s=s: k_ref[0, s:s + kb, sl], lambda vr, s=s: vt_ref[vr, s:s + kb])
              for s in range(0, seq, kb)]
    if has_ctx:
        blocks.append((lambda sl: kc_ref[0, :, sl], lambda vr: vtc_ref[0, vr, :]))

    def scores(h):
        sl = slice(h * LANES, (h + 1) * LANES)
        qh = q_ref[0, :, sl]
        sts = [_dot_nt(keys(sl), qh) for keys, _ in blocks]
        m = functools.reduce(jnp.maximum, [jnp.max(st, axis=0, keepdims=True) for st in sts])
        return h, sts, m

    def values(h, sts, m):
        vr = slice(h * VT_ROWS, (h + 1) * VT_ROWS)
        r = None
        for st, (_, vals) in zip(sts, blocks):
            part = _dot(vals(vr), jnp.exp2(st - m).astype(BF16))
            r = part if r is None else r + part
        return r[:MLA_V] / r[MLA_V:MLA_V + 1]

    outs, pending = [], None
    for h in range(MLA_HEADS):
        cur = scores(h)
        if pending is not None:
            outs.append(values(*pending))
        pending = cur
    outs.append(values(*pending))
    o_ref[0] = jnp.concatenate(outs, axis=0).T.astype(o_ref.dtype)


def _mla_attention(q, k, vt, kc, vtc, *, tq):
    b, seq, _ = q.shape
    in_specs = [pl.BlockSpec((1, tq, 4 * LANES), lambda i, j: (i, j, 0)),
                pl.BlockSpec((1, seq, 4 * LANES), lambda i, j: (i, 0, 0)),
                pl.BlockSpec((MLA_HEADS * VT_ROWS, seq), lambda i, j: (0, i))]
    args = [q, k, vt]
    if kc is not None:
        p = kc.shape[1]
        in_specs += [pl.BlockSpec((1, p, 4 * LANES), lambda i, j: (i, 0, 0)),
                     pl.BlockSpec((1, MLA_HEADS * VT_ROWS, p), lambda i, j: (i, 0, 0))]
        args += [kc, vtc]
    return pl.pallas_call(
        functools.partial(_mla_kernel, has_ctx=kc is not None),
        grid=(b, seq // tq), in_specs=in_specs,
        out_specs=pl.BlockSpec((1, tq, MLA_HEADS * MLA_V), lambda i, j: (i, j, 0)),
        out_shape=jax.ShapeDtypeStruct((b, seq, MLA_HEADS * MLA_V), BF16),
        compiler_params=_params(2), name="mla_attn",
    )(*args)


def _ffn_kernel(*refs, final):
    if final:
        (x_ref, of_ref, obw_ref, z_ref, ng_ref, ob_ref, oc_ref, mod_ref, n2_ref, wo_ref, w1_ref, w2_ref,
         fg_ref, o_ref) = refs
    else:
        (x_ref, of_ref, obw_ref, z_ref, ng_ref, ob_ref, oc_ref, mod_ref, n2_ref, wo_ref, w1_ref, w2_ref,
         o_ref) = refs
    g1 = mod_ref[0, 2:3, :]
    sh2 = mod_ref[0, 3:4, :]
    sc2 = mod_ref[0, 4:5, :]
    g2 = mod_ref[0, 5:6, :]
    mix = _dot(ob_ref[...], wo_ref[512:768, :]) + _dot(oc_ref[...], wo_ref[768:1024, :])
    for h in range(DN_HEADS):
        sl = slice(h * LANES, (h + 1) * LANES)
        z = z_ref[:, sl]
        o = of_ref[:, sl].astype(F32) + obw_ref[:, sl].astype(F32)
        oa = _rms(o, ng_ref[...]) * (z * _sigmoid(z))
        mix = mix + _dot(oa.astype(BF16), wo_ref[sl, :])
    x = x_ref[...] + g1 * mix
    h2 = (_rms(x, n2_ref[...]) * (1.0 + sc2) + sh2).astype(BF16)
    acc = None
    fc = 1024
    for c in range(D_FF // fc):
        a = jnp.maximum(_dot(h2, w1_ref[:, c * fc:(c + 1) * fc]), 0.0)
        part = _dot((a * a).astype(BF16), w2_ref[c * fc:(c + 1) * fc, :])
        acc = part if acc is None else acc + part
    x = x + g2 * acc
    if final:
        x = _rms(x, fg_ref[...])
    o_ref[...] = x


def _ffn(x2d, o_fwd, o_bwd, z, dn_g, ob, oc, mod_l, n2g, wp, final_g, *, latent, seq, tm):
    n = x2d.shape[0]
    per_seq = seq // tm if latent else 1
    rows = lambda w: pl.BlockSpec((tm, w), lambda i: (i, 0))
    if latent:
        mod_spec = pl.BlockSpec((1, N_MOD, D_MODEL), lambda i: (i // per_seq, 0, 0))
    else:
        mod_spec = pl.BlockSpec((1, N_MOD, D_MODEL), lambda i: (MOD_ROWS // 2, 0, 0))
    in_specs = [rows(D_MODEL), rows(512), rows(512), rows(512), _resident((1, DN_DV)), rows(256), rows(256),
                mod_spec, _resident((1, D_MODEL)),
                _resident((D_MODEL, D_MODEL)), _resident((D_MODEL, D_FF)), _resident((D_FF, D_MODEL))]
    args = [x2d, o_fwd, o_bwd, z, dn_g, ob, oc, mod_l, n2g, wp["w_out"], wp["w_ff1"], wp["w_ff2"]]
    if final_g is not None:
        in_specs.append(_resident((1, D_MODEL)))
        args.append(final_g)
    return pl.pallas_call(
        functools.partial(_ffn_kernel, final=final_g is not None),
        grid=(n // tm,), in_specs=in_specs, out_specs=rows(D_MODEL),
        out_shape=jax.ShapeDtypeStruct((n, D_MODEL), F32),
        compiler_params=_params(1), name="ffn",
    )(*args)


def _rope_tables(n_tok, rot_dim, lane0):
    rows = n_tok // GRID_W
    row = np.repeat(np.arange(rows, dtype=np.float32), GRID_W)
    colp = np.tile(np.arange(GRID_W, dtype=np.float32), rows)
    n_freq = rot_dim // 4
    inv_freq = (np.float32(ROPE_BASE) ** (-np.arange(n_freq, dtype=np.float32) / np.float32(n_freq))).astype(np.float32)
    ang = np.concatenate([row[:, None] * inv_freq, colp[:, None] * inv_freq], axis=-1).astype(np.float32)
    cos = np.cos(ang.astype(np.float64)).astype(np.float32)
    sin = np.sin(ang.astype(np.float64)).astype(np.float32)
    half = rot_dim // 2
    zeros = lambda w: np.zeros((n_tok, w), np.float32)
    tail = LANES - lane0 - rot_dim
    c = np.concatenate([np.ones((n_tok, lane0), np.float32), cos, cos, zeros(tail)], axis=1)
    s1 = np.concatenate([zeros(lane0 + half), sin, zeros(tail)], axis=1)
    s2 = np.concatenate([zeros(lane0), -sin, zeros(half + tail)], axis=1)
    return jnp.asarray(c), jnp.asarray(s1), jnp.asarray(s2)


def _pad_heads(w, n_heads, width):
    k = w.shape[0]
    w = w.reshape(k, n_heads, width)
    return jnp.pad(w, ((0, 0), (0, 0), (0, LANES - width))).reshape(k, n_heads * LANES)


def _pack_layer(l, w_in, mla_q_norm_g, mla_w_uq, mla_kv_norm_g, mla_w_ukv, w_out, w_ff1, w_ff2):
    w = w_in[l]
    offs = np.cumsum([0, 1536, 512, 4, 4, 4, 4, MLA_Q_LORA, MLA_KV_LORA, MLA_ROPE, 256, 128, 128])
    piece = lambda i: w[:, offs[i]:offs[i + 1]]
    k = w.shape[0]
    small = jnp.concatenate([piece(2), piece(3), piece(4), piece(5), jnp.zeros((k, KR_LANE - 16), F32),
                             piece(8), jnp.zeros((k, LANES - KR_LANE - MLA_ROPE), F32)], axis=1)
    w_packed = jnp.concatenate([piece(0), piece(1), small, piece(6), piece(7),
                                _pad_heads(piece(9), SWA_HEADS, SWA_HD),
                                _pad_heads(piece(10), SWA_KV_HEADS, SWA_HD), piece(11)], axis=1)
    ukv = mla_w_ukv[l].reshape(MLA_KV_LORA, MLA_HEADS, MLA_NOPE + MLA_V)
    wuk = jnp.pad(ukv[:, :, :MLA_NOPE], ((0, 0), (0, 0), (0, LANES - MLA_NOPE)))
    wuv_t = ukv[:, :, MLA_NOPE:].reshape(MLA_KV_LORA, MLA_HEADS * MLA_V).T
    return dict(
        w_in=w_packed.astype(BF16),
        gq=mla_q_norm_g[l][None, :], gkv=mla_kv_norm_g[l][None, :],
        wuq=_pad_heads(mla_w_uq[l], MLA_HEADS, MLA_NOPE + MLA_ROPE).astype(BF16),
        wuk=wuk.reshape(MLA_KV_LORA, MLA_HEADS * LANES).astype(BF16),
        wuv=wuv_t.astype(BF16),
        w_out=w_out[l].astype(BF16), w_ff1=w_ff1[l].astype(BF16), w_ff2=w_ff2[l].astype(BF16))


def _layer(x2d, mod_l, wp, lp, ctx, final_g, *, batch, seq, latent, tables):
    tm = 512
    outs = _inproj(x2d, mod_l, lp["norm1_g"], wp, tables, latent=latent, seq=seq, tm=tm)
    qkv, z, small, qm, km, vm, qs, ks, vs = outs[:9]
    r3 = lambda a: a.reshape(batch, seq, a.shape[-1])
    prep = _dn_prep(qkv, small, lp["dn_conv_w"], lp["dn_a_log"], lp["dn_dt_bias"], seq=seq)
    o_fwd, o_bwd, state = _dn_scan(prep, ctx["dn"] if latent else None, batch=batch, seq=seq,
                                   want_state=not latent)
    tq = 256
    if latent:
        o_b = _mla_attention(r3(qm), r3(km), vm, ctx["km"], ctx["vm"], tq=tq)
        o_c = _swa_attention(r3(qs), r3(ks), r3(vs), ctx["ks"], ctx["vs"], lp["swa_sink"],
                             windowed=True, tq=tq)
    else:
        o_b = _mla_attention(r3(qm), r3(km), vm, None, None, tq=tq)
        o_c = _swa_attention(r3(qs), r3(ks), r3(vs), None, None, lp["swa_sink"], windowed=False, tq=tq)
    n = batch * seq
    x_new = _ffn(x2d, o_fwd, o_bwd, z, lp["dn_norm_g"], o_b.reshape(n, -1), o_c.reshape(n, -1), mod_l,
                 lp["norm2_g"], wp, final_g, latent=latent, seq=seq, tm=tm)
    if latent:
        return x_new, None
    ckv_n, sv = outs[9], outs[10]
    k_c = ks.reshape(batch, seq, SWA_KV_HEADS, LANES)[..., :SWA_HD]
    new = (state, ckv_n.reshape(batch, seq, MLA_KV_LORA),
           small[:, KR_LANE:KR_LANE + MLA_ROPE].reshape(batch, seq, MLA_ROPE),
           k_c, sv.reshape(batch, seq, SWA_KV_HEADS, SWA_HD))
    return x_new, new


def kernel(x_prompt, x_sample, state_dn, cache_mla_ckv, cache_mla_krope, cache_swa_k, cache_swa_v,
           c, c_ctx, ada_w, ada_b, norm1_g, norm2_g, w_in, dn_conv_w, dn_a_log, dn_dt_bias, dn_norm_g,
           mla_q_norm_g, mla_w_uq, mla_kv_norm_g, mla_w_ukv, swa_sink, w_out, w_ff1, w_ff2, final_norm_g):
    depth = ada_w.shape[0]
    bp, sp, _ = x_prompt.shape
    bs, ss, _ = x_sample.shape
    p_len = cache_mla_ckv.shape[2]
    assert bs <= MOD_ROWS // 2

    half_rows = MOD_ROWS // 2
    cvec = jnp.concatenate([c, jnp.zeros((half_rows - bs, D_MODEL), F32), c_ctx[None, :],
                            jnp.zeros((half_rows - 1, D_MODEL), F32)], axis=0)
    mod = _modulation(cvec, ada_w, ada_b).reshape(depth, MOD_ROWS, N_MOD, D_MODEL)

    packed = [_pack_layer(l, w_in, mla_q_norm_g, mla_w_uq, mla_kv_norm_g, mla_w_ukv, w_out, w_ff1, w_ff2)
              for l in range(depth)]
    lps = [dict(norm1_g=norm1_g[l][None, :], norm2_g=norm2_g[l][None, :], dn_conv_w=dn_conv_w[l],
                dn_a_log=dn_a_log[l], dn_dt_bias=dn_dt_bias[l], dn_norm_g=dn_norm_g[l][None, :],
                swa_sink=swa_sink[l]) for l in range(depth)]
    final_g = final_norm_g[None, :]

    xp = x_prompt.reshape(bp * sp, D_MODEL)
    news = []
    for l in range(depth):
        xp, new = _layer(xp, mod[l], packed[l], lps[l], None, final_g if l == depth - 1 else None,
                         batch=bp, seq=sp, latent=False, tables=None)
        news.append(new)
    y_prompt = xp.reshape(bp, sp, D_MODEL)

    tables = _rope_tables(ss, MLA_ROPE, KR_LANE) + _rope_tables(ss, SWA_HD, 0)
    kr_pad = jnp.pad(cache_mla_krope, ((0, 0), (0, 0), (0, 0), (KR_LANE, LANES - KR_LANE - MLA_ROPE)))
    sk_c = cache_swa_k.reshape(bs, depth, p_len, SWA_KV_HEADS * SWA_HD)
    sv_c = cache_swa_v.reshape(bs, depth, p_len, SWA_KV_HEADS * SWA_HD)
    xs = x_sample.reshape(bs * ss, D_MODEL)
    for l in range(depth):
        km, vm, ks, vs = _cache_prep(l, cache_mla_ckv, kr_pad, sk_c, sv_c, packed[l])
        ctx = dict(dn=state_dn[:, l], km=km, vm=vm, ks=ks, vs=vs)
        xs, _ = _layer(xs, mod[l], packed[l], lps[l], ctx, final_g if l == depth - 1 else None,
                       batch=bs, seq=ss, latent=True, tables=tables)
    y_sample = xs.reshape(bs, ss, D_MODEL)

    stack = lambda i: jnp.stack([nw[i] for nw in news], axis=1)
    return (y_prompt, y_sample, stack(0), stack(1), stack(2), stack(3), stack(4))
```

```python
import functools

import numpy as np
import jax
import jax.numpy as jnp
from jax import lax
from jax.experimental import pallas as pl
from jax.experimental.pallas import tpu as pltpu

F32 = jnp.float32
BF16 = jnp.bfloat16

D_MODEL = 1024
GRID_W = 64
EPS = 1e-6
ROPE_BASE = 10000.0
DN_HEADS = 4
DN_DK = 128
DN_DV = 128
DN_CHUNK = 64
MLA_HEADS = 4
MLA_Q_LORA = 256
MLA_KV_LORA = 128
MLA_NOPE = 64
MLA_ROPE = 32
MLA_V = 64
SWA_HEADS = 4
SWA_KV_HEADS = 2
SWA_HD = 64
WINDOW = 128
D_FF = 4 * D_MODEL
N_MOD = 6
LOG2E = 1.4426950408889634
MLA_SCALE = (MLA_NOPE + MLA_ROPE) ** -0.5
SWA_SCALE = SWA_HD ** -0.5

LANES = 128
MOD_ROWS = 16
NEG_BIG = -1e30
VMEM_LIMIT = 56 * 1024 * 1024

C_QKV = (0, 1536)
C_Z = (1536, 2048)
C_SMALL = (2048, 2176)
C_CQ = (2176, 2432)
C_CKV = (2432, 2560)
C_SQ = (2560, 2816)
C_SK = (2816, 2944)
C_SV = (2944, 3072)
IN_PACKED = 3072
KR_LANE = MLA_NOPE
VT_ROWS = MLA_V + 16


def _params(n_grid):
    return pltpu.CompilerParams(dimension_semantics=("arbitrary",) * n_grid,
                                vmem_limit_bytes=VMEM_LIMIT)


def _resident(shape):
    nd = len(shape)
    return pl.BlockSpec(shape, lambda *_: (0,) * nd, pipeline_mode=pl.Buffered(1))


def _dot(a, b):
    return jnp.dot(a, b, preferred_element_type=F32)


def _dot_nt(a, b):
    return lax.dot_general(a, b, (((1,), (1,)), ((), ())), preferred_element_type=F32)


def _rms(x, g):
    return x * lax.rsqrt(jnp.mean(x * x, axis=-1, keepdims=True) + EPS) * g


def _sigmoid(x):
    return 1.0 / (1.0 + jnp.exp(-x))


def _rope(x, c, s1, s2, half):
    return x * c + pltpu.roll(x, half, 1) * s1 + pltpu.roll(x, LANES - half, 1) * s2


def _mod_kernel(c_ref, w_ref, b_ref, o_ref):
    c = c_ref[...]
    s = (c * _sigmoid(c)).astype(BF16)
    o_ref[0] = _dot(s, w_ref[0].astype(BF16)) + b_ref[0]


def _modulation(cvec, ada_w, ada_b):
    L = ada_w.shape[0]
    nb = 768
    return pl.pallas_call(
        _mod_kernel,
        grid=(L, N_MOD * D_MODEL // nb),
        in_specs=[pl.BlockSpec((MOD_ROWS, D_MODEL), lambda l, j: (0, 0)),
                  pl.BlockSpec((1, D_MODEL, nb), lambda l, j: (l, 0, j)),
                  pl.BlockSpec((1, 1, nb), lambda l, j: (l, 0, j))],
        out_specs=pl.BlockSpec((1, MOD_ROWS, nb), lambda l, j: (l, 0, j)),
        out_shape=jax.ShapeDtypeStruct((L, MOD_ROWS, N_MOD * D_MODEL), F32),
        compiler_params=_params(2),
        name="modulation",
    )(cvec, ada_w, ada_b.reshape(L, 1, N_MOD * D_MODEL))


def _v_variants(v):
    lane = lax.broadcasted_iota(jnp.int32, v.shape, 1)
    lo = lane < SWA_HD
    sw = pltpu.roll(v, SWA_HD, 1)
    zero = jnp.zeros_like(v)
    return jnp.concatenate([jnp.where(lo, v, zero), jnp.where(lo, zero, sw),
                            jnp.where(lo, sw, zero), jnp.where(lo, zero, v)], axis=1)


def _store_vt(vt_ref, vt):
    ones = jnp.ones((VT_ROWS - MLA_V, vt.shape[1]), BF16)
    for h in range(vt.shape[0] // MLA_V):
        vt_ref[h * VT_ROWS:h * VT_ROWS + MLA_V, :] = vt[h * MLA_V:(h + 1) * MLA_V].astype(BF16)
        vt_ref[h * VT_ROWS + MLA_V:(h + 1) * VT_ROWS, :] = ones


def _inproj_kernel(*refs, latent):
    if latent:
        (x_ref, mod_ref, g1_ref, w_ref, gq_ref, wuq_ref, gkv_ref, wuk_ref, wuv_ref, wsv_ref,
         mc_ref, ms1_ref, ms2_ref, sc_ref, ss1_ref, ss2_ref,
         qkv_o, z_o, small_o, qm_o, km_o, vm_o, qs_o, ks_o, vs_o) = refs
    else:
        (x_ref, mod_ref, g1_ref, w_ref, gq_ref, wuq_ref, gkv_ref, wuk_ref, wuv_ref, wsv_ref,
         qkv_o, z_o, small_o, qm_o, km_o, vm_o, qs_o, ks_o, vs_o, ckv_o, sk_o, sv_o) = refs
    x = x_ref[...]
    sh1 = mod_ref[0, 0:1, :]
    sc1 = mod_ref[0, 1:2, :]
    hb = (_rms(x, g1_ref[...]) * (1.0 + sc1) + sh1).astype(BF16)

    def proj(cols):
        return _dot(hb, w_ref[:, cols[0]:cols[1]])

    small = proj(C_SMALL)
    small_o[...] = small
    cq = proj(C_CQ)
    ckv = proj(C_CKV)
    sq = proj(C_SQ)
    sk = proj(C_SK)

    cqn = _rms(cq, gq_ref[...]).astype(BF16)
    q = _dot(cqn, wuq_ref[...])
    ckv_n = _rms(ckv, gkv_ref[...])
    ckv_b = ckv_n.astype(BF16)
    kn = _dot(ckv_b, wuk_ref[...])
    _store_vt(vm_o, _dot_nt(wuv_ref[...], ckv_b))
    qkv_o[...] = proj(C_QKV)
    z_o[...] = proj(C_Z)
    lane = lax.broadcasted_iota(jnp.int32, small.shape, 1)
    kr = jnp.where((lane >= KR_LANE) & (lane < KR_LANE + MLA_ROPE), small, 0.0)
    if latent:
        mc, ms1, ms2 = mc_ref[...], ms1_ref[...], ms2_ref[...]
        kr = _rope(kr, mc, ms1, ms2, MLA_ROPE // 2)
    for h in range(MLA_HEADS):
        sl = slice(h * LANES, (h + 1) * LANES)
        qh = q[:, sl]
        if latent:
            qh = _rope(qh, mc, ms1, ms2, MLA_ROPE // 2)
        qm_o[:, sl] = (qh * (MLA_SCALE * LOG2E)).astype(BF16)
        km_o[:, sl] = (kn[:, sl] + kr).astype(BF16)

    if latent:
        sc, ss1, ss2 = sc_ref[...], ss1_ref[...], ss2_ref[...]
    for t in range(SWA_HEADS * SWA_HD // LANES):
        sl = slice(t * LANES, (t + 1) * LANES)
        qh = sq[:, sl]
        if latent:
            qh = _rope(qh, sc, ss1, ss2, SWA_HD // 2)
        qs_o[:, sl] = (qh * (SWA_SCALE * LOG2E)).astype(BF16)
    kr = _rope(sk, sc, ss1, ss2, SWA_HD // 2) if latent else sk
    ks_o[...] = _v_variants(kr).astype(BF16)
    _store_vt(vs_o, _dot_nt(wsv_ref[...], hb))
    if not latent:
        ckv_o[...] = ckv_n
        sk_o[...] = sk
        sv_o[...] = proj(C_SV)


def _inproj(x2d, mod_l, g1, wp, tables, *, latent, seq, tm):
    n = x2d.shape[0]
    nblk = n // tm
    per_seq = seq // tm if latent else 1
    rows = lambda w: pl.BlockSpec((tm, w), lambda i: (i, 0))
    if latent:
        mod_spec = pl.BlockSpec((1, N_MOD, D_MODEL), lambda i: (i // per_seq, 0, 0))
    else:
        mod_spec = pl.BlockSpec((1, N_MOD, D_MODEL), lambda i: (MOD_ROWS // 2, 0, 0))
    in_specs = [rows(D_MODEL), mod_spec, _resident((1, D_MODEL)), _resident((D_MODEL, IN_PACKED)),
                _resident((1, MLA_Q_LORA)), _resident((MLA_Q_LORA, 4 * LANES)),
                _resident((1, MLA_KV_LORA)), _resident((MLA_KV_LORA, 4 * LANES)),
                _resident((MLA_HEADS * MLA_V, MLA_KV_LORA)), _resident((SWA_KV_HEADS * SWA_HD, D_MODEL))]
    args = [x2d, mod_l, g1, wp["w_in"], wp["gq"], wp["wuq"], wp["gkv"], wp["wuk"], wp["wuv"], wp["wsv_t"]]
    out_shape = [jax.ShapeDtypeStruct((n, 1536), F32), jax.ShapeDtypeStruct((n, 512), F32),
                 jax.ShapeDtypeStruct((n, LANES), F32),
                 jax.ShapeDtypeStruct((n, 512), BF16), jax.ShapeDtypeStruct((n, 512), BF16),
                 jax.ShapeDtypeStruct((MLA_HEADS * VT_ROWS, n), BF16),
                 jax.ShapeDtypeStruct((n, 256), BF16),
                 jax.ShapeDtypeStruct((n, 512), BF16),
                 jax.ShapeDtypeStruct((SWA_KV_HEADS * VT_ROWS, n), BF16)]
    out_specs = [rows(1536), rows(512), rows(LANES), rows(512), rows(512),
                 pl.BlockSpec((MLA_HEADS * VT_ROWS, tm), lambda i: (0, i)),
                 rows(256), rows(512), pl.BlockSpec((SWA_KV_HEADS * VT_ROWS, tm), lambda i: (0, i))]
    if latent:
        tab = pl.BlockSpec((tm, LANES), lambda i: (i % per_seq, 0))
        in_specs += [tab] * 6
        args += list(tables)
    else:
        out_shape += [jax.ShapeDtypeStruct((n, LANES), F32)] * 3
        out_specs += [rows(LANES)] * 3
    return pl.pallas_call(
        functools.partial(_inproj_kernel, latent=latent),
        grid=(nblk,), in_specs=in_specs, out_specs=out_specs, out_shape=out_shape,
        compiler_params=_params(1), name="inproj_lat" if latent else "inproj_ctx",
    )(*args)


def _cache_kernel(ckv_ref, kr_ref, sk_ref, sv_ref, wuk_ref, wuv_ref, km_o, vm_o, ks_o, vs_o):
    ckv = ckv_ref[0, 0].astype(BF16)
    kn = _dot(ckv, wuk_ref[...])
    kr = kr_ref[0, 0]
    for h in range(MLA_HEADS):
        sl = slice(h * LANES, (h + 1) * LANES)
        km_o[0, :, sl] = (kn[:, sl] + kr).astype(BF16)
    _store_vt(vm_o.at[0], _dot_nt(wuv_ref[...], ckv))
    sk = sk_ref[0, 0]
    ks_o[0] = _v_variants(sk).astype(BF16)
    _store_vt(vs_o.at[0], sv_ref[0, 0].T)


def _cache_prep(layer, ckv, kr_pad, sk, sv, wp):
    b, _, p, _ = ckv.shape
    cin = pl.BlockSpec((1, 1, p, LANES), lambda i: (i, layer, 0, 0))
    out = lambda w: pl.BlockSpec((1, p, w), lambda i: (i, 0, 0))
    return pl.pallas_call(
        _cache_kernel, grid=(b,),
        in_specs=[cin, cin, cin, cin, _resident((MLA_KV_LORA, 4 * LANES)),
                  _resident((MLA_HEADS * MLA_V, MLA_KV_LORA))],
        out_specs=[out(512), pl.BlockSpec((1, MLA_HEADS * VT_ROWS, p), lambda i: (i, 0, 0)), out(512),
                   pl.BlockSpec((1, SWA_KV_HEADS * VT_ROWS, p), lambda i: (i, 0, 0))],
        out_shape=[jax.ShapeDtypeStruct((b, p, 512), BF16),
                   jax.ShapeDtypeStruct((b, MLA_HEADS * VT_ROWS, p), BF16),
                   jax.ShapeDtypeStruct((b, p, 512), BF16),
                   jax.ShapeDtypeStruct((b, SWA_KV_HEADS * VT_ROWS, p), BF16)],
        compiler_params=_params(1), name="cache_prep",
    )(ckv, kr_pad, sk, sv, wp["wuk"], wp["wuv"])


PAIR = 2 * DN_CHUNK


def _bmm(a, b):
    return jnp.einsum("bij,bjk->bik", a, b, preferred_element_type=F32)


def _bmm_nt(a, b):
    return jnp.einsum("bik,bjk->bij", a, b, preferred_element_type=F32)


def _split(a):
    hi = a.astype(BF16)
    return hi, (a - hi.astype(F32)).astype(BF16)


def _block_diag(x):
    first = (lax.broadcasted_iota(jnp.int32, x.shape, 2) & (LANES - 1)) < DN_CHUNK
    zero = jnp.zeros_like(x)
    return jnp.concatenate([jnp.where(first, x, zero), jnp.where(first, zero, x)], axis=1)


def _mm_packed(a, b, split):
    if not split:
        return _bmm(a.astype(BF16), _block_diag(b.astype(BF16)))
    c = a.shape[1]
    ah, al = _split(a)
    bh, bl = _split(b)
    lhs = jnp.concatenate([jnp.concatenate([ah, ah], axis=2), jnp.concatenate([al, al], axis=2)], axis=1)
    r = _bmm(lhs, jnp.concatenate([_block_diag(bh), _block_diag(bl)], axis=1))
    return r[:, :c] + r[:, c:]


def _tri_inverse_packed(low):
    c, n = low.shape[1], low.shape[2]
    eye = (lax.broadcasted_iota(jnp.int32, (c, n), 0)
           == (lax.broadcasted_iota(jnp.int32, (c, n), 1) & (c - 1))).astype(F32)
    pp = -low
    tt = eye + pp
    pp = _mm_packed(pp, pp, True)
    s = 2
    while 2 * s < c:
        r = _mm_packed(pp, jnp.concatenate([pp, tt], axis=2), 2 * s < c // 2)
        pp = r[:, :, :n]
        tt = tt + r[:, :, n:]
        s *= 2
    return tt + _mm_packed(pp, tt, False)


def _chunk_cumsum(x, forward):
    rows = x.shape[0]
    rc = lax.broadcasted_iota(jnp.int32, x.shape, 0) & (DN_CHUNK - 1)
    s = 1
    while s < DN_CHUNK:
        if forward:
            x = x + jnp.where(rc >= s, pltpu.roll(x, s, 0), 0.0)
        else:
            x = x + jnp.where(rc < DN_CHUNK - s, pltpu.roll(x, rows - s, 0), 0.0)
        s *= 2
    return x


def _dn_prep_kernel(qkv_ref, prev_ref, next_ref, small_ref, cw_ref, alog_ref, dtb_ref,
                    u_o, w_o, qg_o, kdt_o, intra_o, eg_o, *, tt, tiles_per_seq):
    C = DN_CHUNK
    i = pl.program_id(0)
    first = (i % tiles_per_seq) == 0
    last = (i % tiles_per_seq) == tiles_per_seq - 1
    row = lax.broadcasted_iota(jnp.int32, (tt, LANES), 0)
    n_chunks = tt // C
    hw = DN_HEADS * LANES

    def conv(c0):
        sl = slice(c0, c0 + LANES)
        x = qkv_ref[:, sl]
        prev_row = jnp.where(first, 0.0, prev_ref[7:8, sl])
        next_row = jnp.where(last, 0.0, next_ref[0:1, sl])
        xp = jnp.where(row == 0, prev_row, pltpu.roll(x, 1, 0))
        xn = jnp.where(row == tt - 1, next_row, pltpu.roll(x, tt - 1, 0))
        y = xp * cw_ref[0:1, sl] + x * cw_ref[1:2, sl] + xn * cw_ref[2:3, sl]
        return y * _sigmoid(y)

    small = small_ref[...]
    ri = lax.broadcasted_iota(jnp.int32, (C, LANES), 0)
    cj = lax.broadcasted_iota(jnp.int32, (C, LANES), 1)
    cm = cj & (C - 1)
    incl = (ri >= cm, ri <= cm)
    strict = (ri > cm, ri < cm)
    half = (cj < C, cj >= C)

    qs, ks, vs = [], [], []
    for j in range(DN_HEADS):
        q = conv(j * LANES)
        k = conv(hw + j * LANES)
        vs.append(conv(2 * hw + j * LANES))
        qs.append(q * lax.rsqrt(jnp.sum(q * q, -1, keepdims=True) + EPS) * (DN_DK ** -0.5))
        ks.append(k * lax.rsqrt(jnp.sum(k * k, -1, keepdims=True) + EPS))

    lane_t = lax.broadcasted_iota(jnp.int32, (tt, LANES), 1)
    xa = small + dtb_ref[...]
    softplus = jnp.maximum(xa, 0.0) + jnp.log(1.0 + jnp.exp(-jnp.abs(xa)))
    g_all = -jnp.exp(alog_ref[...]) * softplus
    beta_all = _sigmoid(small)
    gc_all = jnp.where(lane_t < DN_HEADS, _chunk_cumsum(g_all, True), _chunk_cumsum(g_all, False))
    gc_rows = gc_all.T

    ch = lambda a, c: a[c * C:(c + 1) * C]
    pr = lambda a, m: a[m * PAIR:(m + 1) * PAIR]
    hp = [(j, m) for j in range(DN_HEADS) for m in range(n_chunks // 2)]
    kq = jnp.stack([jnp.concatenate([pr(ks[j], m), pr(qs[j], m)], axis=0).astype(BF16) for j, m in hp])
    kk = jnp.stack([pr(ks[j], m).astype(BF16) for j, m in hp])
    gq = _bmm_nt(kq, kk)

    chains = [(j, m, d) for j, m in hp for d in range(2)]
    lows, intras, sol_rhs = [], [], []
    gcb = {}
    for j, m, d in chains:
        b = hp.index((j, m))
        ln = DN_HEADS * d + j
        bl = 2 * DN_HEADS + ln
        cols = slice(j * LANES, (j + 1) * LANES)
        for c in (2 * m, 2 * m + 1):
            gcb[(j, c, d)] = jnp.broadcast_to(ch(gc_all, c)[:, ln:ln + 1], (C, LANES))
        gc_p = jnp.where(half[0], gcb[(j, 2 * m, d)], gcb[(j, 2 * m + 1, d)])
        beta_p = jnp.where(half[0], jnp.broadcast_to(ch(beta_all, 2 * m)[:, bl:bl + 1], (C, LANES)),
                           jnp.broadcast_to(ch(beta_all, 2 * m + 1)[:, bl:bl + 1], (C, LANES)))
        key_row = gc_rows[ln:ln + 1, m * PAIR:(m + 1) * PAIR]
        decay = jnp.exp(jnp.where(incl[d], gc_p - key_row, NEG_BIG))
        gram = jnp.where(half[0], gq[b, 0:C], gq[b, C:PAIR])
        qk = jnp.where(half[0], gq[b, PAIR:PAIR + C], gq[b, PAIR + C:])
        lows.append(jnp.where(strict[d], gram * beta_p * decay, 0.0))
        intra = jnp.where(incl[d], qk * decay, 0.0)
        intra_o[d, (2 * m) * C:(2 * m + 1) * C, cols] = jnp.where(half[0], intra, 0.0).astype(BF16)
        intra_o[d, (2 * m + 1) * C:(2 * m + 2) * C, cols] = jnp.where(half[0], 0.0, intra).astype(BF16)
        rhs = []
        for c in (2 * m, 2 * m + 1):
            beta = ch(beta_all, c)[:, bl:bl + 1]
            kc = ch(ks[j], c)
            rhs.append(jnp.concatenate([ch(vs[j], c) * beta, kc * beta * jnp.exp(gcb[(j, c, d)])], axis=1))
        sol_rhs.append(jnp.concatenate(rhs, axis=0).astype(BF16))
    tinv = _tri_inverse_packed(jnp.stack(lows))

    th, tl = _split(tinv)
    first_half = lax.broadcasted_iota(jnp.int32, th.shape, 2) < C
    zero = jnp.zeros_like(th)
    sol = _bmm(jnp.concatenate([jnp.where(first_half, th, zero), jnp.where(first_half, tl, zero),
                                jnp.where(first_half, zero, th), jnp.where(first_half, zero, tl)], axis=1),
               jnp.stack(sol_rhs))

    for n, (j, m, d) in enumerate(chains):
        cols = slice(j * LANES, (j + 1) * LANES)
        kdec = []
        for i, c in enumerate((2 * m, 2 * m + 1)):
            rows = slice(c * C, (c + 1) * C)
            uw = sol[n, 2 * i * C:(2 * i + 1) * C] + sol[n, (2 * i + 1) * C:(2 * i + 2) * C]
            u_o[d, rows, cols] = uw[:, :DN_DV].astype(BF16)
            w_o[d, rows, cols] = uw[:, DN_DV:].astype(BF16)
            gc = gcb[(j, c, d)]
            qg_o[d, rows, cols] = (ch(qs[j], c) * jnp.exp(gc)).astype(BF16)
            g_last = gc[C - 1:C, :] if d == 0 else gc[0:1, :]
            kdec.append(ch(ks[j], c) * jnp.exp(g_last - gc))
            eg_o[d, c * 8:(c + 1) * 8, cols] = jnp.broadcast_to(jnp.exp(g_last), (8, LANES))
        kdt_o[d, m * PAIR:(m + 1) * PAIR, cols] = jnp.concatenate(kdec, axis=0).T.astype(BF16)


def _dn_prep(qkv, small, conv_w, a_log, dt_bias, *, seq):
    n = qkv.shape[0]
    tt = 256
    t8 = tt // 8
    last8 = n // 8 - 1
    hw = DN_HEADS * LANES
    lane_vec = lambda a: jnp.pad(a.reshape(1, 2 * DN_HEADS), ((0, 0), (0, LANES - 2 * DN_HEADS)))
    out = lambda rows: pl.BlockSpec((2, rows, hw), lambda i: (0, i, 0))
    big = lambda dt: jax.ShapeDtypeStruct((2, n, hw), dt)
    return pl.pallas_call(
        functools.partial(_dn_prep_kernel, tt=tt, tiles_per_seq=seq // tt),
        grid=(n // tt,),
        in_specs=[pl.BlockSpec((tt, 3 * hw), lambda i: (i, 0)),
                  pl.BlockSpec((8, 3 * hw), lambda i: (jnp.maximum(i * t8 - 1, 0), 0)),
                  pl.BlockSpec((8, 3 * hw), lambda i: (jnp.minimum((i + 1) * t8, last8), 0)),
                  pl.BlockSpec((tt, LANES), lambda i: (i, 0)),
                  _resident((3, 3 * hw)), _resident((1, LANES)), _resident((1, LANES))],
        out_specs=[out(tt)] * 5 + [out(t8)],
        out_shape=[big(BF16), big(BF16), big(BF16), big(BF16), big(BF16),
                   jax.ShapeDtypeStruct((2, n // 8, hw), F32)],
        compiler_params=_params(1), name="dn_prep",
    )(qkv, qkv, qkv, small, conv_w, lane_vec(a_log), lane_vec(dt_bias))


def _dn_scan_kernel(*refs, ts, bb, has_s0, want_state):
    it = iter(refs)
    ops = [[next(it) for _ in range(6)] for _ in range(2)]
    s0_ref = next(it) if has_s0 else None
    o_refs = [next(it), next(it)]
    sfin_ref = next(it) if want_state else None
    st = next(it)
    i = pl.program_id(1)
    chains = [(g, d, h) for g in range(bb) for d in range(2) for h in range(DN_HEADS)]

    @pl.when(i == 0)
    def _():
        for n, (g, d, h) in enumerate(chains):
            st[n] = s0_ref[g, d, h] if has_s0 else jnp.zeros((DN_DK, DN_DV), F32)

    n_chunks = ts // DN_CHUNK
    zeros = jnp.zeros((DN_CHUNK, DN_DV), F32)
    S = st[...]
    for step in range(n_chunks):
        chunk = lambda d: step if d == 0 else n_chunks - 1 - step
        wq, us, intras, kdts, egs = [], [], [], [], []
        for g, d, h in chains:
            u_ref, w_ref, qg_ref, kdt_ref, intra_ref, eg_ref = ops[d]
            c = chunk(d)
            rows = slice(c * DN_CHUNK, (c + 1) * DN_CHUNK)
            prow = slice((c // 2) * PAIR, (c // 2 + 1) * PAIR)
            cols = slice(h * LANES, (h + 1) * LANES)
            wq.append(jnp.concatenate([w_ref[0, g, rows, cols], qg_ref[0, g, rows, cols]], axis=0))
            us.append(u_ref[0, g, rows, cols])
            intras.append(intra_ref[0, g, rows, cols])
            kdts.append(kdt_ref[0, g, prow, cols])
            egs.append(eg_ref[0, g, c * 8:c * 8 + 1, cols])
        r = _bmm(jnp.stack(wq), S.astype(BF16))
        v_new = jnp.stack(us) - r[:, :DN_CHUNK]
        v_ext = jnp.stack([jnp.concatenate([v_new[n], zeros] if chunk(d) % 2 == 0 else [zeros, v_new[n]], axis=0)
                           for n, (g, d, h) in enumerate(chains)]).astype(BF16)
        o = r[:, DN_CHUNK:] + _bmm(jnp.stack(intras), v_ext)
        S = S * jnp.stack(egs) + _bmm(jnp.stack(kdts), v_ext)
        for n, (g, d, h) in enumerate(chains):
            c = chunk(d)
            o_refs[d][g, c * DN_CHUNK:(c + 1) * DN_CHUNK, h * LANES:(h + 1) * LANES] = o[n].astype(BF16)
    st[...] = S

    if want_state:
        @pl.when(i == pl.num_programs(1) - 1)
        def _():
            for n, (g, d, h) in enumerate(chains):
                sfin_ref[g, d, h] = S[n]


def _dn_scan(prep, s0, *, batch, seq, want_state):
    ts = min(seq, 512)
    nt = seq // ts
    bb = 2
    hw = DN_HEADS * LANES
    in_specs, args = [], []
    for d in range(2):
        imap = (lambda b, i: (0, b, i, 0)) if d == 0 else (lambda b, i: (1, b, nt - 1 - i, 0))
        for a in prep:
            rows = a.shape[1] // batch
            in_specs.append(pl.BlockSpec((1, bb, rows // nt, hw), imap))
            args.append(a.reshape(2, batch, rows, hw))
    state_spec = pl.BlockSpec((bb, 2, DN_HEADS, DN_DK, DN_DV), lambda b, i: (b, 0, 0, 0, 0))
    if s0 is not None:
        in_specs.append(state_spec)
        args.append(s0)
    out_specs = [pl.BlockSpec((bb, ts, hw), lambda b, i: (b, i, 0)),
                 pl.BlockSpec((bb, ts, hw), lambda b, i: (b, nt - 1 - i, 0))]
    out_shape = [jax.ShapeDtypeStruct((batch, seq, hw), BF16)] * 2
    if want_state:
        out_specs.append(state_spec)
        out_shape.append(jax.ShapeDtypeStruct((batch, 2, DN_HEADS, DN_DK, DN_DV), F32))
    res = pl.pallas_call(
        functools.partial(_dn_scan_kernel, ts=ts, bb=bb, has_s0=s0 is not None, want_state=want_state),
        grid=(batch // bb, nt), in_specs=in_specs, out_specs=out_specs, out_shape=out_shape,
        scratch_shapes=[pltpu.VMEM((2 * bb * DN_HEADS, DN_DK, DN_DV), F32)],
        compiler_params=_params(2), name="dn_scan",
    )(*args)
    return (res[0].reshape(batch * seq, hw), res[1].reshape(batch * seq, hw), res[2] if want_state else None)


def _attn_kernel(*refs, tq, seq, windowed, has_ctx):
    it = iter(refs)
    qm_ref, km_ref, vtm_ref = next(it), next(it), next(it)
    kcm_ref, vtcm_ref = (next(it), next(it)) if has_ctx else (None, None)
    qs_ref, ks_ref, vts_ref = next(it), next(it), next(it)
    kcs_ref, vtcs_ref = (next(it), next(it)) if has_ctx else (None, None)
    sink_ref = next(it)
    om_ref, os_ref = next(it), next(it)

    kb = min(seq, 512)
    blocks = [(lambda sl, s=s: km_ref[0, s:s + kb, sl], lambda vr, s=s: vtm_ref[vr, s:s + kb])
              for s in range(0, seq, kb)]
    if has_ctx:
        blocks.append((lambda sl: kcm_ref[0, :, sl], lambda vr: vtcm_ref[0, vr, :]))

    def mla_scores(h):
        sl = slice(h * LANES, (h + 1) * LANES)
        qh = qm_ref[0, :, sl]
        sts = [_dot_nt(keys(sl), qh) for keys, _ in blocks]
        m = functools.reduce(jnp.maximum, [jnp.max(st, axis=0, keepdims=True) for st in sts])
        return h, sts, m

    def mla_values(h, sts, m):
        vr = slice(h * VT_ROWS, (h + 1) * VT_ROWS)
        r = None
        for st, (_, vals) in zip(sts, blocks):
            part = _dot(vals(vr), jnp.exp2(st - m).astype(BF16))
            r = part if r is None else r + part
        return r[:MLA_V] / r[MLA_V:MLA_V + 1]

    q0 = pl.program_id(1) * tq
    if windowed:
        wl = tq + 2 * WINDOW
        ws = pl.multiple_of(jnp.clip(q0 - WINDOW, 0, seq - wl), WINDOW)
        kpos = ws + lax.broadcasted_iota(jnp.int32, (wl, tq), 0)
        qpos = q0 + lax.broadcasted_iota(jnp.int32, (wl, tq), 1)
        valid = jnp.abs(qpos - kpos) <= WINDOW
        keys_s = pl.ds(ws, wl)
    else:
        keys_s = slice(None)

    def swa_scores(h):
        qt = qs_ref[0, :, (h // 2) * LANES:(h // 2 + 1) * LANES]
        st = _dot_nt(ks_ref[0, keys_s, h * LANES:(h + 1) * LANES], qt)
        if windowed:
            st = jnp.where(valid, st, NEG_BIG)
        sc = _dot_nt(kcs_ref[0, :, h * LANES:(h + 1) * LANES], qt) if has_ctx else None
        return h, st, sc

    def swa_values(h, st, sc):
        sink = sink_ref[h] * LOG2E
        vr = slice((h // 2) * VT_ROWS, (h // 2 + 1) * VT_ROWS)
        m = jnp.maximum(jnp.max(st, axis=0, keepdims=True), sink)
        if has_ctx:
            m = jnp.maximum(m, jnp.max(sc, axis=0, keepdims=True))
        r = _dot(vts_ref[vr, keys_s], jnp.exp2(st - m).astype(BF16))
        if has_ctx:
            r = r + _dot(vtcs_ref[0, vr, :], jnp.exp2(sc - m).astype(BF16))
        return r[:SWA_HD] / (r[SWA_HD:SWA_HD + 1] + jnp.exp2(sink - m))

    units = ([(mla_scores, mla_values, h) for h in range(MLA_HEADS)]
             + [(swa_scores, swa_values, h) for h in range(SWA_HEADS)])
    outs, pending = [], None
    for score_fn, value_fn, h in units:
        cur = (value_fn, score_fn(h))
        if pending is not None:
            outs.append(pending[0](*pending[1]))
        pending = cur
    outs.append(pending[0](*pending[1]))
    om_ref[0] = jnp.concatenate(outs[:MLA_HEADS], axis=0).T.astype(om_ref.dtype)
    os_ref[0] = jnp.concatenate(outs[MLA_HEADS:], axis=0).T.astype(os_ref.dtype)


def _attention(qm, km, vtm, qs, ks, vts, ctx, sink, *, windowed, tq):
    b, seq, _ = qm.shape
    per_q = lambda a: pl.BlockSpec((1, tq, a.shape[2]), lambda i, j: (i, j, 0))
    per_b = lambda a: pl.BlockSpec((1,) + a.shape[1:], lambda i, j: (i, 0, 0))
    cols_b = lambda a: pl.BlockSpec((a.shape[0], seq), lambda i, j: (0, i))
    in_specs, args = [], []
    for q, k, vt, kc, vtc in ((qm, km, vtm, "km", "vm"), (qs, ks, vts, "ks", "vs")):
        in_specs += [per_q(q), per_b(k), cols_b(vt)]
        args += [q, k, vt]
        if ctx is not None:
            in_specs += [per_b(ctx[kc]), per_b(ctx[vtc])]
            args += [ctx[kc], ctx[vtc]]
    in_specs.append(pl.BlockSpec(memory_space=pltpu.SMEM))
    args.append(sink)
    width = MLA_HEADS * MLA_V
    return pl.pallas_call(
        functools.partial(_attn_kernel, tq=tq, seq=seq, windowed=windowed, has_ctx=ctx is not None),
        grid=(b, seq // tq), in_specs=in_specs,
        out_specs=[pl.BlockSpec((1, tq, width), lambda i, j: (i, j, 0))] * 2,
        out_shape=[jax.ShapeDtypeStruct((b, seq, width), BF16)] * 2,
        compiler_params=_params(2), name="attention",
    )(*args)


def _ffn_kernel(*refs, final):
    if final:
        (x_ref, of_ref, obw_ref, z_ref, ng_ref, ob_ref, oc_ref, mod_ref, n2_ref, wo_ref, w1_ref, w2_ref,
         fg_ref, o_ref) = refs
    else:
        (x_ref, of_ref, obw_ref, z_ref, ng_ref, ob_ref, oc_ref, mod_ref, n2_ref, wo_ref, w1_ref, w2_ref,
         o_ref) = refs
    g1 = mod_ref[0, 2:3, :]
    sh2 = mod_ref[0, 3:4, :]
    sc2 = mod_ref[0, 4:5, :]
    g2 = mod_ref[0, 5:6, :]
    mix = _dot(ob_ref[...], wo_ref[512:768, :]) + _dot(oc_ref[...], wo_ref[768:1024, :])
    for h in range(DN_HEADS):
        sl = slice(h * LANES, (h + 1) * LANES)
        z = z_ref[:, sl]
        o = of_ref[:, sl].astype(F32) + obw_ref[:, sl].astype(F32)
        oa = _rms(o, ng_ref[...]) * (z * _sigmoid(z))
        mix = mix + _dot(oa.astype(BF16), wo_ref[sl, :])
    x = x_ref[...] + g1 * mix
    h2 = (_rms(x, n2_ref[...]) * (1.0 + sc2) + sh2).astype(BF16)
    acc = None
    fc = 1024
    for c in range(D_FF // fc):
        a = jnp.maximum(_dot(h2, w1_ref[:, c * fc:(c + 1) * fc]), 0.0)
        part = _dot((a * a).astype(BF16), w2_ref[c * fc:(c + 1) * fc, :])
        acc = part if acc is None else acc + part
    x = x + g2 * acc
    if final:
        x = _rms(x, fg_ref[...])
    o_ref[...] = x


def _ffn(x2d, o_fwd, o_bwd, z, dn_g, ob, oc, mod_l, n2g, wp, final_g, *, latent, seq, tm):
    n = x2d.shape[0]
    per_seq = seq // tm if latent else 1
    rows = lambda w: pl.BlockSpec((tm, w), lambda i: (i, 0))
    if latent:
        mod_spec = pl.BlockSpec((1, N_MOD, D_MODEL), lambda i: (i // per_seq, 0, 0))
    else:
        mod_spec = pl.BlockSpec((1, N_MOD, D_MODEL), lambda i: (MOD_ROWS // 2, 0, 0))
    in_specs = [rows(D_MODEL), rows(512), rows(512), rows(512), _resident((1, DN_DV)), rows(256), rows(256),
                mod_spec, _resident((1, D_MODEL)),
                _resident((D_MODEL, D_MODEL)), _resident((D_MODEL, D_FF)), _resident((D_FF, D_MODEL))]
    args = [x2d, o_fwd, o_bwd, z, dn_g, ob, oc, mod_l, n2g, wp["w_out"], wp["w_ff1"], wp["w_ff2"]]
    if final_g is not None:
        in_specs.append(_resident((1, D_MODEL)))
        args.append(final_g)
    return pl.pallas_call(
        functools.partial(_ffn_kernel, final=final_g is not None),
        grid=(n // tm,), in_specs=in_specs, out_specs=rows(D_MODEL),
        out_shape=jax.ShapeDtypeStruct((n, D_MODEL), F32),
        compiler_params=_params(1), name="ffn",
    )(*args)


def _rope_tables(n_tok, rot_dim, lane0, both_halves=False):
    rows = n_tok // GRID_W
    row = np.repeat(np.arange(rows, dtype=np.float32), GRID_W)
    colp = np.tile(np.arange(GRID_W, dtype=np.float32), rows)
    n_freq = rot_dim // 4
    inv_freq = (np.float32(ROPE_BASE) ** (-np.arange(n_freq, dtype=np.float32) / np.float32(n_freq))).astype(np.float32)
    ang = np.concatenate([row[:, None] * inv_freq, colp[:, None] * inv_freq], axis=-1).astype(np.float32)
    cos = np.cos(ang.astype(np.float64)).astype(np.float32)
    sin = np.sin(ang.astype(np.float64)).astype(np.float32)
    half = rot_dim // 2
    zeros = lambda w: np.zeros((n_tok, w), np.float32)
    tail = LANES - lane0 - rot_dim
    c = np.concatenate([np.ones((n_tok, lane0), np.float32), cos, cos, zeros(tail)], axis=1)
    s1 = np.concatenate([zeros(lane0 + half), sin, zeros(tail)], axis=1)
    s2 = np.concatenate([zeros(lane0), -sin, zeros(half + tail)], axis=1)
    if both_halves:
        c, s1, s2 = (np.concatenate([a[:, :LANES // 2]] * 2, axis=1) for a in (c, s1, s2))
    return jnp.asarray(c), jnp.asarray(s1), jnp.asarray(s2)


def _pad_heads(w, n_heads, width):
    k = w.shape[0]
    w = w.reshape(k, n_heads, width)
    return jnp.pad(w, ((0, 0), (0, 0), (0, LANES - width))).reshape(k, n_heads * LANES)


def _pack_layer(l, w_in, mla_q_norm_g, mla_w_uq, mla_kv_norm_g, mla_w_ukv, w_out, w_ff1, w_ff2):
    w = w_in[l]
    offs = np.cumsum([0, 1536, 512, 4, 4, 4, 4, MLA_Q_LORA, MLA_KV_LORA, MLA_ROPE, 256, 128, 128])
    piece = lambda i: w[:, offs[i]:offs[i + 1]]
    k = w.shape[0]
    small = jnp.concatenate([piece(2), piece(3), piece(4), piece(5), jnp.zeros((k, KR_LANE - 16), F32),
                             piece(8), jnp.zeros((k, LANES - KR_LANE - MLA_ROPE), F32)], axis=1)
    w_packed = jnp.concatenate([piece(0), piece(1), small, piece(6), piece(7),
                                piece(9), piece(10), piece(11)], axis=1)
    ukv = mla_w_ukv[l].reshape(MLA_KV_LORA, MLA_HEADS, MLA_NOPE + MLA_V)
    wuk = jnp.pad(ukv[:, :, :MLA_NOPE], ((0, 0), (0, 0), (0, LANES - MLA_NOPE)))
    wuv_t = ukv[:, :, MLA_NOPE:].reshape(MLA_KV_LORA, MLA_HEADS * MLA_V).T
    return dict(
        w_in=w_packed.astype(BF16),
        gq=mla_q_norm_g[l][None, :], gkv=mla_kv_norm_g[l][None, :],
        wuq=_pad_heads(mla_w_uq[l], MLA_HEADS, MLA_NOPE + MLA_ROPE).astype(BF16),
        wuk=wuk.reshape(MLA_KV_LORA, MLA_HEADS * LANES).astype(BF16),
        wuv=wuv_t.astype(BF16), wsv_t=piece(11).T.astype(BF16),
        w_out=w_out[l].astype(BF16), w_ff1=w_ff1[l].astype(BF16), w_ff2=w_ff2[l].astype(BF16))


def _layer(x2d, mod_l, wp, lp, ctx, final_g, *, batch, seq, latent, tables):
    tm = 512
    outs = _inproj(x2d, mod_l, lp["norm1_g"], wp, tables, latent=latent, seq=seq, tm=tm)
    qkv, z, small, qm, km, vm, qs, ks, vs = outs[:9]
    r3 = lambda a: a.reshape(batch, seq, a.shape[-1])
    prep = _dn_prep(qkv, small, lp["dn_conv_w"], lp["dn_a_log"], lp["dn_dt_bias"], seq=seq)
    o_fwd, o_bwd, state = _dn_scan(prep, ctx["dn"] if latent else None, batch=batch, seq=seq,
                                   want_state=not latent)
    o_b, o_c = _attention(r3(qm), r3(km), vm, r3(qs), r3(ks), vs, ctx if latent else None, lp["swa_sink"],
                          windowed=latent, tq=256)
    n = batch * seq
    x_new = _ffn(x2d, o_fwd, o_bwd, z, lp["dn_norm_g"], o_b.reshape(n, -1), o_c.reshape(n, -1), mod_l,
                 lp["norm2_g"], wp, final_g, latent=latent, seq=seq, tm=tm)
    if latent:
        return x_new, None
    ckv_n, sk, sv = outs[9], outs[10], outs[11]
    new = (state, ckv_n.reshape(batch, seq, MLA_KV_LORA),
           small[:, KR_LANE:KR_LANE + MLA_ROPE].reshape(batch, seq, MLA_ROPE),
           sk.reshape(batch, seq, SWA_KV_HEADS, SWA_HD), sv.reshape(batch, seq, SWA_KV_HEADS, SWA_HD))
    return x_new, new


def kernel(x_prompt, x_sample, state_dn, cache_mla_ckv, cache_mla_krope, cache_swa_k, cache_swa_v,
           c, c_ctx, ada_w, ada_b, norm1_g, norm2_g, w_in, dn_conv_w, dn_a_log, dn_dt_bias, dn_norm_g,
           mla_q_norm_g, mla_w_uq, mla_kv_norm_g, mla_w_ukv, swa_sink, w_out, w_ff1, w_ff2, final_norm_g):
    depth = ada_w.shape[0]
    bp, sp, _ = x_prompt.shape
    bs, ss, _ = x_sample.shape
    p_len = cache_mla_ckv.shape[2]
    assert bs <= MOD_ROWS // 2

    half_rows = MOD_ROWS // 2
    cvec = jnp.concatenate([c, jnp.zeros((half_rows - bs, D_MODEL), F32), c_ctx[None, :],
                            jnp.zeros((half_rows - 1, D_MODEL), F32)], axis=0)
    mod = _modulation(cvec, ada_w, ada_b).reshape(depth, MOD_ROWS, N_MOD, D_MODEL)

    packed = [_pack_layer(l, w_in, mla_q_norm_g, mla_w_uq, mla_kv_norm_g, mla_w_ukv, w_out, w_ff1, w_ff2)
              for l in range(depth)]
    lps = [dict(norm1_g=norm1_g[l][None, :], norm2_g=norm2_g[l][None, :], dn_conv_w=dn_conv_w[l],
                dn_a_log=dn_a_log[l], dn_dt_bias=dn_dt_bias[l], dn_norm_g=dn_norm_g[l][None, :],
                swa_sink=swa_sink[l]) for l in range(depth)]
    final_g = final_norm_g[None, :]

    xp = x_prompt.reshape(bp * sp, D_MODEL)
    news = []
    for l in range(depth):
        xp, new = _layer(xp, mod[l], packed[l], lps[l], None, final_g if l == depth - 1 else None,
                         batch=bp, seq=sp, latent=False, tables=None)
        news.append(new)
    y_prompt = xp.reshape(bp, sp, D_MODEL)

    tables = _rope_tables(ss, MLA_ROPE, KR_LANE) + _rope_tables(ss, SWA_HD, 0, both_halves=True)
    kr_pad = jnp.pad(cache_mla_krope, ((0, 0), (0, 0), (0, 0), (KR_LANE, LANES - KR_LANE - MLA_ROPE)))
    sk_c = cache_swa_k.reshape(bs, depth, p_len, SWA_KV_HEADS * SWA_HD)
    sv_c = cache_swa_v.reshape(bs, depth, p_len, SWA_KV_HEADS * SWA_HD)
    xs = x_sample.reshape(bs * ss, D_MODEL)
    for l in range(depth):
        km, vm, ks, vs = _cache_prep(l, cache_mla_ckv, kr_pad, sk_c, sv_c, packed[l])
        ctx = dict(dn=state_dn[:, l], km=km, vm=vm, ks=ks, vs=vs)
        xs, _ = _layer(xs, mod[l], packed[l], lps[l], ctx, final_g if l == depth - 1 else None,
                       batch=bs, seq=ss, latent=True, tables=tables)
    y_sample = xs.reshape(bs, ss, D_MODEL)

    stack = lambda i: jnp.stack([nw[i] for nw in news], axis=1)
    return (y_prompt, y_sample, stack(0), stack(1), stack(2), stack(3), stack(4))
```

```python
import functools

import numpy as np
import jax
import jax.numpy as jnp
from jax import lax
from jax.experimental import pallas as pl
from jax.experimental.pallas import tpu as pltpu

F32 = jnp.float32
BF16 = jnp.bfloat16

D_MODEL = 1024
GRID_W = 64
EPS = 1e-6
ROPE_BASE = 10000.0
DN_HEADS = 4
DN_DK = 128
DN_DV = 128
DN_CHUNK = 64
MLA_HEADS = 4
MLA_Q_LORA = 256
MLA_KV_LORA = 128
MLA_NOPE = 64
MLA_ROPE = 32
MLA_V = 64
SWA_HEADS = 4
SWA_KV_HEADS = 2
SWA_HD = 64
WINDOW = 128
D_FF = 4 * D_MODEL
N_MOD = 6
LOG2E = 1.4426950408889634
MLA_SCALE = (MLA_NOPE + MLA_ROPE) ** -0.5
SWA_SCALE = SWA_HD ** -0.5

LANES = 128
MOD_ROWS = 16
NEG_BIG = -1e30
VMEM_LIMIT = 56 * 1024 * 1024

C_QKV = (0, 1536)
C_Z = (1536, 2048)
C_SMALL = (2048, 2176)
C_CQ = (2176, 2432)
C_CKV = (2432, 2560)
C_SQ = (2560, 2816)
C_SK = (2816, 2944)
C_SV = (2944, 3072)
IN_PACKED = 3072
KR_LANE = MLA_NOPE
VT_ROWS = MLA_V + 16


def _params(n_grid):
    return pltpu.CompilerParams(dimension_semantics=("arbitrary",) * n_grid,
                                vmem_limit_bytes=VMEM_LIMIT)


def _resident(shape):
    nd = len(shape)
    return pl.BlockSpec(shape, lambda *_: (0,) * nd, pipeline_mode=pl.Buffered(1))


def _dot(a, b):
    return jnp.dot(a, b, preferred_element_type=F32)


def _dot_nt(a, b):
    return lax.dot_general(a, b, (((1,), (1,)), ((), ())), preferred_element_type=F32)


def _rms(x, g):
    return x * lax.rsqrt(jnp.mean(x * x, axis=-1, keepdims=True) + EPS) * g


def _sigmoid(x):
    return 1.0 / (1.0 + jnp.exp(-x))


def _rope(x, c, s1, s2, half):
    return x * c + pltpu.roll(x, half, 1) * s1 + pltpu.roll(x, LANES - half, 1) * s2


def _mod_kernel(c_ref, w_ref, b_ref, o_ref):
    c = c_ref[...]
    s = (c * _sigmoid(c)).astype(BF16)
    o_ref[0] = _dot(s, w_ref[0].astype(BF16)) + b_ref[0]


def _modulation(cvec, ada_w, ada_b):
    L = ada_w.shape[0]
    nb = 768
    return pl.pallas_call(
        _mod_kernel,
        grid=(L, N_MOD * D_MODEL // nb),
        in_specs=[pl.BlockSpec((MOD_ROWS, D_MODEL), lambda l, j: (0, 0)),
                  pl.BlockSpec((1, D_MODEL, nb), lambda l, j: (l, 0, j)),
                  pl.BlockSpec((1, 1, nb), lambda l, j: (l, 0, j))],
        out_specs=pl.BlockSpec((1, MOD_ROWS, nb), lambda l, j: (l, 0, j)),
        out_shape=jax.ShapeDtypeStruct((L, MOD_ROWS, N_MOD * D_MODEL), F32),
        compiler_params=_params(2),
        name="modulation",
    )(cvec, ada_w, ada_b.reshape(L, 1, N_MOD * D_MODEL))


def _v_variants(v):
    lane = lax.broadcasted_iota(jnp.int32, v.shape, 1)
    lo = lane < SWA_HD
    sw = pltpu.roll(v, SWA_HD, 1)
    zero = jnp.zeros_like(v)
    return jnp.concatenate([jnp.where(lo, v, zero), jnp.where(lo, zero, sw),
                            jnp.where(lo, sw, zero), jnp.where(lo, zero, v)], axis=1)


def _store_vt(vt_ref, vt):
    ones = jnp.ones((VT_ROWS - MLA_V, vt.shape[1]), BF16)
    for h in range(vt.shape[0] // MLA_V):
        vt_ref[h * VT_ROWS:h * VT_ROWS + MLA_V, :] = vt[h * MLA_V:(h + 1) * MLA_V].astype(BF16)
        vt_ref[h * VT_ROWS + MLA_V:(h + 1) * VT_ROWS, :] = ones


def _inproj_kernel(*refs, latent):
    if latent:
        (x_ref, mod_ref, g1_ref, w_ref, gq_ref, wuq_ref, gkv_ref, wuk_ref, wuv_ref, wsv_ref,
         mc_ref, ms1_ref, ms2_ref, sc_ref, ss1_ref, ss2_ref,
         qkv_o, z_o, small_o, qm_o, km_o, vm_o, qs_o, ks_o, vs_o) = refs
    else:
        (x_ref, mod_ref, g1_ref, w_ref, gq_ref, wuq_ref, gkv_ref, wuk_ref, wuv_ref, wsv_ref,
         qkv_o, z_o, small_o, qm_o, km_o, vm_o, qs_o, ks_o, vs_o, ckv_o, sk_o, sv_o) = refs
    x = x_ref[...]
    sh1 = mod_ref[0, 0:1, :]
    sc1 = mod_ref[0, 1:2, :]
    hb = (_rms(x, g1_ref[...]) * (1.0 + sc1) + sh1).astype(BF16)

    def proj(cols):
        return _dot(hb, w_ref[:, cols[0]:cols[1]])

    small = proj(C_SMALL)
    small_o[...] = small
    cq = proj(C_CQ)
    ckv = proj(C_CKV)
    sq = proj(C_SQ)
    sk = proj(C_SK)

    cqn = _rms(cq, gq_ref[...]).astype(BF16)
    q = _dot(cqn, wuq_ref[...])
    ckv_n = _rms(ckv, gkv_ref[...])
    ckv_b = ckv_n.astype(BF16)
    kn = _dot(ckv_b, wuk_ref[...])
    _store_vt(vm_o, _dot_nt(wuv_ref[...], ckv_b))
    qkv_o[...] = proj(C_QKV)
    z_o[...] = proj(C_Z)
    lane = lax.broadcasted_iota(jnp.int32, small.shape, 1)
    kr = jnp.where((lane >= KR_LANE) & (lane < KR_LANE + MLA_ROPE), small, 0.0)
    if latent:
        mc, ms1, ms2 = mc_ref[...], ms1_ref[...], ms2_ref[...]
        kr = _rope(kr, mc, ms1, ms2, MLA_ROPE // 2)
    for h in range(MLA_HEADS):
        sl = slice(h * LANES, (h + 1) * LANES)
        qh = q[:, sl]
        if latent:
            qh = _rope(qh, mc, ms1, ms2, MLA_ROPE // 2)
        qm_o[:, sl] = (qh * (MLA_SCALE * LOG2E)).astype(BF16)
        km_o[:, sl] = (kn[:, sl] + kr).astype(BF16)

    if latent:
        sc, ss1, ss2 = sc_ref[...], ss1_ref[...], ss2_ref[...]
    for t in range(SWA_HEADS * SWA_HD // LANES):
        sl = slice(t * LANES, (t + 1) * LANES)
        qh = sq[:, sl]
        if latent:
            qh = _rope(qh, sc, ss1, ss2, SWA_HD // 2)
        qs_o[:, sl] = (qh * (SWA_SCALE * LOG2E)).astype(BF16)
    kr = _rope(sk, sc, ss1, ss2, SWA_HD // 2) if latent else sk
    ks_o[...] = _v_variants(kr).astype(BF16)
    _store_vt(vs_o, _dot_nt(wsv_ref[...], hb))
    if not latent:
        ckv_o[...] = ckv_n
        sk_o[...] = sk
        sv_o[...] = proj(C_SV)


def _inproj(x2d, mod_l, g1, wp, tables, *, latent, seq, tm):
    n = x2d.shape[0]
    nblk = n // tm
    per_seq = seq // tm if latent else 1
    rows = lambda w: pl.BlockSpec((tm, w), lambda i: (i, 0))
    if latent:
        mod_spec = pl.BlockSpec((1, N_MOD, D_MODEL), lambda i: (i // per_seq, 0, 0))
    else:
        mod_spec = pl.BlockSpec((1, N_MOD, D_MODEL), lambda i: (MOD_ROWS // 2, 0, 0))
    in_specs = [rows(D_MODEL), mod_spec, _resident((1, D_MODEL)), _resident((D_MODEL, IN_PACKED)),
                _resident((1, MLA_Q_LORA)), _resident((MLA_Q_LORA, 4 * LANES)),
                _resident((1, MLA_KV_LORA)), _resident((MLA_KV_LORA, 4 * LANES)),
                _resident((MLA_HEADS * MLA_V, MLA_KV_LORA)), _resident((SWA_KV_HEADS * SWA_HD, D_MODEL))]
    args = [x2d, mod_l, g1, wp["w_in"], wp["gq"], wp["wuq"], wp["gkv"], wp["wuk"], wp["wuv"], wp["wsv_t"]]
    out_shape = [jax.ShapeDtypeStruct((n, 1536), F32), jax.ShapeDtypeStruct((n, 512), F32),
                 jax.ShapeDtypeStruct((n, LANES), F32),
                 jax.ShapeDtypeStruct((n, 512), BF16), jax.ShapeDtypeStruct((n, 512), BF16),
                 jax.ShapeDtypeStruct((MLA_HEADS * VT_ROWS, n), BF16),
                 jax.ShapeDtypeStruct((n, 256), BF16),
                 jax.ShapeDtypeStruct((n, 512), BF16),
                 jax.ShapeDtypeStruct((SWA_KV_HEADS * VT_ROWS, n), BF16)]
    out_specs = [rows(1536), rows(512), rows(LANES), rows(512), rows(512),
                 pl.BlockSpec((MLA_HEADS * VT_ROWS, tm), lambda i: (0, i)),
                 rows(256), rows(512), pl.BlockSpec((SWA_KV_HEADS * VT_ROWS, tm), lambda i: (0, i))]
    if latent:
        tab = pl.BlockSpec((tm, LANES), lambda i: (i % per_seq, 0))
        in_specs += [tab] * 6
        args += list(tables)
    else:
        out_shape += [jax.ShapeDtypeStruct((n, LANES), F32)] * 3
        out_specs += [rows(LANES)] * 3
    return pl.pallas_call(
        functools.partial(_inproj_kernel, latent=latent),
        grid=(nblk,), in_specs=in_specs, out_specs=out_specs, out_shape=out_shape,
        compiler_params=_params(1), name="inproj_lat" if latent else "inproj_ctx",
    )(*args)


def _cache_kernel(ckv_ref, kr_ref, sk_ref, sv_ref, wuk_ref, wuv_ref, km_o, vm_o, ks_o, vs_o):
    ckv = ckv_ref[0, 0].astype(BF16)
    kn = _dot(ckv, wuk_ref[...])
    kr = kr_ref[0, 0]
    for h in range(MLA_HEADS):
        sl = slice(h * LANES, (h + 1) * LANES)
        km_o[0, :, sl] = (kn[:, sl] + kr).astype(BF16)
    _store_vt(vm_o.at[0], _dot_nt(wuv_ref[...], ckv))
    sk = sk_ref[0, 0]
    ks_o[0] = _v_variants(sk).astype(BF16)
    _store_vt(vs_o.at[0], sv_ref[0, 0].T)


def _cache_prep(layer, ckv, kr_pad, sk, sv, wp):
    b, _, p, _ = ckv.shape
    cin = pl.BlockSpec((1, 1, p, LANES), lambda i: (i, layer, 0, 0))
    out = lambda w: pl.BlockSpec((1, p, w), lambda i: (i, 0, 0))
    return pl.pallas_call(
        _cache_kernel, grid=(b,),
        in_specs=[cin, cin, cin, cin, _resident((MLA_KV_LORA, 4 * LANES)),
                  _resident((MLA_HEADS * MLA_V, MLA_KV_LORA))],
        out_specs=[out(512), pl.BlockSpec((1, MLA_HEADS * VT_ROWS, p), lambda i: (i, 0, 0)), out(512),
                   pl.BlockSpec((1, SWA_KV_HEADS * VT_ROWS, p), lambda i: (i, 0, 0))],
        out_shape=[jax.ShapeDtypeStruct((b, p, 512), BF16),
                   jax.ShapeDtypeStruct((b, MLA_HEADS * VT_ROWS, p), BF16),
                   jax.ShapeDtypeStruct((b, p, 512), BF16),
                   jax.ShapeDtypeStruct((b, SWA_KV_HEADS * VT_ROWS, p), BF16)],
        compiler_params=_params(1), name="cache_prep",
    )(ckv, kr_pad, sk, sv, wp["wuk"], wp["wuv"])


PAIR = 2 * DN_CHUNK


def _bmm(a, b):
    return jnp.einsum("bij,bjk->bik", a, b, preferred_element_type=F32)


def _bmm_nt(a, b):
    return jnp.einsum("bik,bjk->bij", a, b, preferred_element_type=F32)


def _split(a):
    hi = a.astype(BF16)
    return hi, (a - hi.astype(F32)).astype(BF16)


def _block_diag(x):
    first = (lax.broadcasted_iota(jnp.int32, x.shape, 2) & (LANES - 1)) < DN_CHUNK
    zero = jnp.zeros_like(x)
    return jnp.concatenate([jnp.where(first, x, zero), jnp.where(first, zero, x)], axis=1)


def _mm_packed(a, b, split):
    if not split:
        return _bmm(a.astype(BF16), _block_diag(b.astype(BF16)))
    c = a.shape[1]
    ah, al = _split(a)
    bh, bl = _split(b)
    lhs = jnp.concatenate([jnp.concatenate([ah, ah], axis=2), jnp.concatenate([al, al], axis=2)], axis=1)
    r = _bmm(lhs, jnp.concatenate([_block_diag(bh), _block_diag(bl)], axis=1))
    return r[:, :c] + r[:, c:]


def _tri_inverse_packed(low):
    c, n = low.shape[1], low.shape[2]
    eye = (lax.broadcasted_iota(jnp.int32, (c, n), 0)
           == (lax.broadcasted_iota(jnp.int32, (c, n), 1) & (c - 1))).astype(F32)
    pp = -low
    tt = eye + pp
    pp = _mm_packed(pp, pp, True)
    s = 2
    while 2 * s < c:
        r = _mm_packed(pp, jnp.concatenate([pp, tt], axis=2), 2 * s < c // 2)
        pp = r[:, :, :n]
        tt = tt + r[:, :, n:]
        s *= 2
    return tt + _mm_packed(pp, tt, False)


def _chunk_cumsum(x, forward):
    rows = x.shape[0]
    rc = lax.broadcasted_iota(jnp.int32, x.shape, 0) & (DN_CHUNK - 1)
    s = 1
    while s < DN_CHUNK:
        if forward:
            x = x + jnp.where(rc >= s, pltpu.roll(x, s, 0), 0.0)
        else:
            x = x + jnp.where(rc < DN_CHUNK - s, pltpu.roll(x, rows - s, 0), 0.0)
        s *= 2
    return x


def _dn_prep_kernel(qkv_ref, prev_ref, next_ref, small_ref, cw_ref, alog_ref, dtb_ref,
                    u_o, w_o, qg_o, kdt_o, intra_o, eg_o, *, tt, tiles_per_seq):
    C = DN_CHUNK
    i = pl.program_id(0)
    first = (i % tiles_per_seq) == 0
    last = (i % tiles_per_seq) == tiles_per_seq - 1
    row = lax.broadcasted_iota(jnp.int32, (tt, LANES), 0)
    n_chunks = tt // C
    hw = DN_HEADS * LANES

    def conv(c0):
        sl = slice(c0, c0 + LANES)
        x = qkv_ref[:, sl]
        prev_row = jnp.where(first, 0.0, prev_ref[7:8, sl])
        next_row = jnp.where(last, 0.0, next_ref[0:1, sl])
        xp = jnp.where(row == 0, prev_row, pltpu.roll(x, 1, 0))
        xn = jnp.where(row == tt - 1, next_row, pltpu.roll(x, tt - 1, 0))
        y = xp * cw_ref[0:1, sl] + x * cw_ref[1:2, sl] + xn * cw_ref[2:3, sl]
        return y * _sigmoid(y)

    small = small_ref[...]
    ri = lax.broadcasted_iota(jnp.int32, (C, LANES), 0)
    cj = lax.broadcasted_iota(jnp.int32, (C, LANES), 1)
    cm = cj & (C - 1)
    incl = (ri >= cm, ri <= cm)
    strict = (ri > cm, ri < cm)
    half = (cj < C, cj >= C)

    qs, ks, vs = [], [], []
    for j in range(DN_HEADS):
        q = conv(j * LANES)
        k = conv(hw + j * LANES)
        vs.append(conv(2 * hw + j * LANES))
        qs.append(q * lax.rsqrt(jnp.sum(q * q, -1, keepdims=True) + EPS) * (DN_DK ** -0.5))
        ks.append(k * lax.rsqrt(jnp.sum(k * k, -1, keepdims=True) + EPS))

    lane_t = lax.broadcasted_iota(jnp.int32, (tt, LANES), 1)
    xa = small + dtb_ref[...]
    softplus = jnp.maximum(xa, 0.0) + jnp.log(1.0 + jnp.exp(-jnp.abs(xa)))
    g_all = -jnp.exp(alog_ref[...]) * softplus
    beta_all = _sigmoid(small)
    gc_all = jnp.where(lane_t < DN_HEADS, _chunk_cumsum(g_all, True), _chunk_cumsum(g_all, False))
    gc_rows = gc_all.T

    ch = lambda a, c: a[c * C:(c + 1) * C]
    pr = lambda a, m: a[m * PAIR:(m + 1) * PAIR]
    hp = [(j, m) for j in range(DN_HEADS) for m in range(n_chunks // 2)]
    kq = jnp.stack([jnp.concatenate([pr(ks[j], m), pr(qs[j], m)], axis=0).astype(BF16) for j, m in hp])
    kk = jnp.stack([pr(ks[j], m).astype(BF16) for j, m in hp])
    gq = _bmm_nt(kq, kk)

    chains = [(j, m, d) for j, m in hp for d in range(2)]
    lows, intras, sol_rhs = [], [], []
    gcb = {}
    for j, m, d in chains:
        b = hp.index((j, m))
        ln = DN_HEADS * d + j
        bl = 2 * DN_HEADS + ln
        cols = slice(j * LANES, (j + 1) * LANES)
        for c in (2 * m, 2 * m + 1):
            gcb[(j, c, d)] = jnp.broadcast_to(ch(gc_all, c)[:, ln:ln + 1], (C, LANES))
        gc_p = jnp.where(half[0], gcb[(j, 2 * m, d)], gcb[(j, 2 * m + 1, d)])
        beta_p = jnp.where(half[0], jnp.broadcast_to(ch(beta_all, 2 * m)[:, bl:bl + 1], (C, LANES)),
                           jnp.broadcast_to(ch(beta_all, 2 * m + 1)[:, bl:bl + 1], (C, LANES)))
        key_row = gc_rows[ln:ln + 1, m * PAIR:(m + 1) * PAIR]
        decay = jnp.exp(jnp.where(incl[d], gc_p - key_row, NEG_BIG))
        gram = jnp.where(half[0], gq[b, 0:C], gq[b, C:PAIR])
        qk = jnp.where(half[0], gq[b, PAIR:PAIR + C], gq[b, PAIR + C:])
        lows.append(jnp.where(strict[d], gram * beta_p * decay, 0.0))
        intra = jnp.where(incl[d], qk * decay, 0.0)
        intra_o[d, (2 * m) * C:(2 * m + 1) * C, cols] = jnp.where(half[0], intra, 0.0).astype(BF16)
        intra_o[d, (2 * m + 1) * C:(2 * m + 2) * C, cols] = jnp.where(half[0], 0.0, intra).astype(BF16)
        rhs = []
        for c in (2 * m, 2 * m + 1):
            beta = ch(beta_all, c)[:, bl:bl + 1]
            kc = ch(ks[j], c)
            rhs.append(jnp.concatenate([ch(vs[j], c) * beta, kc * beta * jnp.exp(gcb[(j, c, d)])], axis=1))
        sol_rhs.append(jnp.concatenate(rhs, axis=0).astype(BF16))
    tinv = _tri_inverse_packed(jnp.stack(lows))

    th, tl = _split(tinv)
    first_half = lax.broadcasted_iota(jnp.int32, th.shape, 2) < C
    zero = jnp.zeros_like(th)
    sol = _bmm(jnp.concatenate([jnp.where(first_half, th, zero), jnp.where(first_half, tl, zero),
                                jnp.where(first_half, zero, th), jnp.where(first_half, zero, tl)], axis=1),
               jnp.stack(sol_rhs))

    for n, (j, m, d) in enumerate(chains):
        cols = slice(j * LANES, (j + 1) * LANES)
        kdec = []
        for i, c in enumerate((2 * m, 2 * m + 1)):
            rows = slice(c * C, (c + 1) * C)
            uw = sol[n, 2 * i * C:(2 * i + 1) * C] + sol[n, (2 * i + 1) * C:(2 * i + 2) * C]
            u_o[d, rows, cols] = uw[:, :DN_DV].astype(BF16)
            w_o[d, rows, cols] = uw[:, DN_DV:].astype(BF16)
            gc = gcb[(j, c, d)]
            qg_o[d, rows, cols] = (ch(qs[j], c) * jnp.exp(gc)).astype(BF16)
            g_last = gc[C - 1:C, :] if d == 0 else gc[0:1, :]
            kdec.append(ch(ks[j], c) * jnp.exp(g_last - gc))
            eg_o[d, c * 8:(c + 1) * 8, cols] = jnp.broadcast_to(jnp.exp(g_last), (8, LANES))
        kdt_o[d, m * PAIR:(m + 1) * PAIR, cols] = jnp.concatenate(kdec, axis=0).T.astype(BF16)


def _dn_prep(qkv, small, conv_w, a_log, dt_bias, *, seq):
    n = qkv.shape[0]
    tt = min(seq, 512)
    t8 = tt // 8
    last8 = n // 8 - 1
    hw = DN_HEADS * LANES
    lane_vec = lambda a: jnp.pad(a.reshape(1, 2 * DN_HEADS), ((0, 0), (0, LANES - 2 * DN_HEADS)))
    out = lambda rows: pl.BlockSpec((2, rows, hw), lambda i: (0, i, 0))
    big = lambda dt: jax.ShapeDtypeStruct((2, n, hw), dt)
    return pl.pallas_call(
        functools.partial(_dn_prep_kernel, tt=tt, tiles_per_seq=seq // tt),
        grid=(n // tt,),
        in_specs=[pl.BlockSpec((tt, 3 * hw), lambda i: (i, 0)),
                  pl.BlockSpec((8, 3 * hw), lambda i: (jnp.maximum(i * t8 - 1, 0), 0)),
                  pl.BlockSpec((8, 3 * hw), lambda i: (jnp.minimum((i + 1) * t8, last8), 0)),
                  pl.BlockSpec((tt, LANES), lambda i: (i, 0)),
                  _resident((3, 3 * hw)), _resident((1, LANES)), _resident((1, LANES))],
        out_specs=[out(tt)] * 5 + [out(t8)],
        out_shape=[big(BF16), big(BF16), big(BF16), big(BF16), big(BF16),
                   jax.ShapeDtypeStruct((2, n // 8, hw), F32)],
        compiler_params=_params(1), name="dn_prep",
    )(qkv, qkv, qkv, small, conv_w, lane_vec(a_log), lane_vec(dt_bias))


def _dn_scan_kernel(*refs, ts, bb, has_s0, want_state):
    it = iter(refs)
    ops = [[next(it) for _ in range(6)] for _ in range(2)]
    s0_ref = next(it) if has_s0 else None
    o_refs = [next(it), next(it)]
    sfin_ref = next(it) if want_state else None
    st = next(it)
    i = pl.program_id(1)
    chains = [(g, d, h) for g in range(bb) for d in range(2) for h in range(DN_HEADS)]

    @pl.when(i == 0)
    def _():
        for n, (g, d, h) in enumerate(chains):
            st[n] = s0_ref[g, d, h] if has_s0 else jnp.zeros((DN_DK, DN_DV), F32)

    n_chunks = ts // DN_CHUNK
    zeros = jnp.zeros((DN_CHUNK, DN_DV), F32)
    S = st[...]
    for step in range(n_chunks):
        chunk = lambda d: step if d == 0 else n_chunks - 1 - step
        wq, us, intras, kdts, egs = [], [], [], [], []
        for g, d, h in chains:
            u_ref, w_ref, qg_ref, kdt_ref, intra_ref, eg_ref = ops[d]
            c = chunk(d)
            rows = slice(c * DN_CHUNK, (c + 1) * DN_CHUNK)
            prow = slice((c // 2) * PAIR, (c // 2 + 1) * PAIR)
            cols = slice(h * LANES, (h + 1) * LANES)
            wq.append(jnp.concatenate([w_ref[0, g, rows, cols], qg_ref[0, g, rows, cols]], axis=0))
            us.append(u_ref[0, g, rows, cols])
            intras.append(intra_ref[0, g, rows, cols])
            kdts.append(kdt_ref[0, g, prow, cols])
            egs.append(eg_ref[0, g, c * 8:c * 8 + 1, cols])
        r = _bmm(jnp.stack(wq), S.astype(BF16))
        v_new = jnp.stack(us) - r[:, :DN_CHUNK]
        v_ext = jnp.stack([jnp.concatenate([v_new[n], zeros] if chunk(d) % 2 == 0 else [zeros, v_new[n]], axis=0)
                           for n, (g, d, h) in enumerate(chains)]).astype(BF16)
        o = r[:, DN_CHUNK:] + _bmm(jnp.stack(intras), v_ext)
        S = S * jnp.stack(egs) + _bmm(jnp.stack(kdts), v_ext)
        for n, (g, d, h) in enumerate(chains):
            c = chunk(d)
            o_refs[d][g, c * DN_CHUNK:(c + 1) * DN_CHUNK, h * LANES:(h + 1) * LANES] = o[n].astype(BF16)
    st[...] = S

    if want_state:
        @pl.when(i == pl.num_programs(1) - 1)
        def _():
            for n, (g, d, h) in enumerate(chains):
                sfin_ref[g, d, h] = S[n]


def _dn_scan(prep, s0, *, batch, seq, want_state):
    ts = min(seq, 512)
    nt = seq // ts
    bb = 2
    hw = DN_HEADS * LANES
    in_specs, args = [], []
    for d in range(2):
        imap = (lambda b, i: (0, b, i, 0)) if d == 0 else (lambda b, i: (1, b, nt - 1 - i, 0))
        for a in prep:
            rows = a.shape[1] // batch
            in_specs.append(pl.BlockSpec((1, bb, rows // nt, hw), imap))
            args.append(a.reshape(2, batch, rows, hw))
    state_spec = pl.BlockSpec((bb, 2, DN_HEADS, DN_DK, DN_DV), lambda b, i: (b, 0, 0, 0, 0))
    if s0 is not None:
        in_specs.append(state_spec)
        args.append(s0)
    out_specs = [pl.BlockSpec((bb, ts, hw), lambda b, i: (b, i, 0)),
                 pl.BlockSpec((bb, ts, hw), lambda b, i: (b, nt - 1 - i, 0))]
    out_shape = [jax.ShapeDtypeStruct((batch, seq, hw), BF16)] * 2
    if want_state:
        out_specs.append(state_spec)
        out_shape.append(jax.ShapeDtypeStruct((batch, 2, DN_HEADS, DN_DK, DN_DV), F32))
    res = pl.pallas_call(
        functools.partial(_dn_scan_kernel, ts=ts, bb=bb, has_s0=s0 is not None, want_state=want_state),
        grid=(batch // bb, nt), in_specs=in_specs, out_specs=out_specs, out_shape=out_shape,
        scratch_shapes=[pltpu.VMEM((2 * bb * DN_HEADS, DN_DK, DN_DV), F32)],
        compiler_params=_params(2), name="dn_scan",
    )(*args)
    return (res[0].reshape(batch * seq, hw), res[1].reshape(batch * seq, hw), res[2] if want_state else None)


def _attn_kernel(*refs, tq, seq, windowed, has_ctx):
    it = iter(refs)
    qm_ref, km_ref, vtm_ref = next(it), next(it), next(it)
    kcm_ref, vtcm_ref = (next(it), next(it)) if has_ctx else (None, None)
    qs_ref, ks_ref, vts_ref = next(it), next(it), next(it)
    kcs_ref, vtcs_ref = (next(it), next(it)) if has_ctx else (None, None)
    sink_ref = next(it)
    om_ref, os_ref = next(it), next(it)

    kb = min(seq, 512)
    blocks = [(lambda sl, s=s: km_ref[0, s:s + kb, sl], lambda vr, s=s: vtm_ref[vr, s:s + kb])
              for s in range(0, seq, kb)]
    if has_ctx:
        blocks.append((lambda sl: kcm_ref[0, :, sl], lambda vr: vtcm_ref[0, vr, :]))

    def mla_scores(h):
        sl = slice(h * LANES, (h + 1) * LANES)
        qh = qm_ref[0, :, sl]
        sts = [_dot_nt(keys(sl), qh) for keys, _ in blocks]
        m = functools.reduce(jnp.maximum, [jnp.max(st, axis=0, keepdims=True) for st in sts])
        return h, sts, m

    def mla_values(h, sts, m):
        vr = slice(h * VT_ROWS, (h + 1) * VT_ROWS)
        r = None
        for st, (_, vals) in zip(sts, blocks):
            part = _dot(vals(vr), jnp.exp2(st - m).astype(BF16))
            r = part if r is None else r + part
        return r[:MLA_V] / r[MLA_V:MLA_V + 1]

    q0 = pl.program_id(1) * tq
    if windowed:
        wl = tq + 2 * WINDOW
        ws = pl.multiple_of(jnp.clip(q0 - WINDOW, 0, seq - wl), WINDOW)
        kpos = ws + lax.broadcasted_iota(jnp.int32, (wl, tq), 0)
        qpos = q0 + lax.broadcasted_iota(jnp.int32, (wl, tq), 1)
        valid = jnp.abs(qpos - kpos) <= WINDOW
        keys_s = pl.ds(ws, wl)
    else:
        keys_s = slice(None)

    def swa_scores(h):
        qt = qs_ref[0, :, (h // 2) * LANES:(h // 2 + 1) * LANES]
        st = _dot_nt(ks_ref[0, keys_s, h * LANES:(h + 1) * LANES], qt)
        if windowed:
            st = jnp.where(valid, st, NEG_BIG)
        sc = _dot_nt(kcs_ref[0, :, h * LANES:(h + 1) * LANES], qt) if has_ctx else None
        return h, st, sc

    def swa_values(h, st, sc):
        sink = sink_ref[h] * LOG2E
        vr = slice((h // 2) * VT_ROWS, (h // 2 + 1) * VT_ROWS)
        m = jnp.maximum(jnp.max(st, axis=0, keepdims=True), sink)
        if has_ctx:
            m = jnp.maximum(m, jnp.max(sc, axis=0, keepdims=True))
        r = _dot(vts_ref[vr, keys_s], jnp.exp2(st - m).astype(BF16))
        if has_ctx:
            r = r + _dot(vtcs_ref[0, vr, :], jnp.exp2(sc - m).astype(BF16))
        return r[:SWA_HD] / (r[SWA_HD:SWA_HD + 1] + jnp.exp2(sink - m))

    units = ([(mla_scores, mla_values, h) for h in range(MLA_HEADS)]
             + [(swa_scores, swa_values, h) for h in range(SWA_HEADS)])
    outs, pending = [], None
    for score_fn, value_fn, h in units:
        cur = (value_fn, score_fn(h))
        if pending is not None:
            outs.append(pending[0](*pending[1]))
        pending = cur
    outs.append(pending[0](*pending[1]))
    om_ref[0] = jnp.concatenate(outs[:MLA_HEADS], axis=0).T.astype(om_ref.dtype)
    os_ref[0] = jnp.concatenate(outs[MLA_HEADS:], axis=0).T.astype(os_ref.dtype)


def _attention(qm, km, vtm, qs, ks, vts, ctx, sink, *, windowed, tq):
    b, seq, _ = qm.shape
    per_q = lambda a: pl.BlockSpec((1, tq, a.shape[2]), lambda i, j: (i, j, 0))
    per_b = lambda a: pl.BlockSpec((1,) + a.shape[1:], lambda i, j: (i, 0, 0))
    cols_b = lambda a: pl.BlockSpec((a.shape[0], seq), lambda i, j: (0, i))
    in_specs, args = [], []
    for q, k, vt, kc, vtc in ((qm, km, vtm, "km", "vm"), (qs, ks, vts, "ks", "vs")):
        in_specs += [per_q(q), per_b(k), cols_b(vt)]
        args += [q, k, vt]
        if ctx is not None:
            in_specs += [per_b(ctx[kc]), per_b(ctx[vtc])]
            args += [ctx[kc], ctx[vtc]]
    in_specs.append(pl.BlockSpec(memory_space=pltpu.SMEM))
    args.append(sink)
    width = MLA_HEADS * MLA_V
    return pl.pallas_call(
        functools.partial(_attn_kernel, tq=tq, seq=seq, windowed=windowed, has_ctx=ctx is not None),
        grid=(b, seq // tq), in_specs=in_specs,
        out_specs=[pl.BlockSpec((1, tq, width), lambda i, j: (i, j, 0))] * 2,
        out_shape=[jax.ShapeDtypeStruct((b, seq, width), BF16)] * 2,
        compiler_params=_params(2), name="attention",
    )(*args)


def _ffn_kernel(*refs, final):
    if final:
        (x_ref, of_ref, obw_ref, z_ref, ng_ref, ob_ref, oc_ref, mod_ref, n2_ref, wo_ref, w1_ref, w2_ref,
         fg_ref, o_ref) = refs
    else:
        (x_ref, of_ref, obw_ref, z_ref, ng_ref, ob_ref, oc_ref, mod_ref, n2_ref, wo_ref, w1_ref, w2_ref,
         o_ref) = refs
    g1 = mod_ref[0, 2:3, :]
    sh2 = mod_ref[0, 3:4, :]
    sc2 = mod_ref[0, 4:5, :]
    g2 = mod_ref[0, 5:6, :]
    mix = _dot(ob_ref[...], wo_ref[512:768, :]) + _dot(oc_ref[...], wo_ref[768:1024, :])
    for h in range(DN_HEADS):
        sl = slice(h * LANES, (h + 1) * LANES)
        z = z_ref[:, sl]
        o = of_ref[:, sl].astype(F32) + obw_ref[:, sl].astype(F32)
        oa = _rms(o, ng_ref[...]) * (z * _sigmoid(z))
        mix = mix + _dot(oa.astype(BF16), wo_ref[sl, :])
    x = x_ref[...] + g1 * mix
    h2 = (_rms(x, n2_ref[...]) * (1.0 + sc2) + sh2).astype(BF16)
    acc = None
    fc = 1024
    for c in range(D_FF // fc):
        a = jnp.maximum(_dot(h2, w1_ref[:, c * fc:(c + 1) * fc]), 0.0)
        part = _dot((a * a).astype(BF16), w2_ref[c * fc:(c + 1) * fc, :])
        acc = part if acc is None else acc + part
    x = x + g2 * acc
    if final:
        x = _rms(x, fg_ref[...])
    o_ref[...] = x


def _ffn(x2d, o_fwd, o_bwd, z, dn_g, ob, oc, mod_l, n2g, wp, final_g, *, latent, seq, tm):
    n = x2d.shape[0]
    per_seq = seq // tm if latent else 1
    rows = lambda w: pl.BlockSpec((tm, w), lambda i: (i, 0))
    if latent:
        mod_spec = pl.BlockSpec((1, N_MOD, D_MODEL), lambda i: (i // per_seq, 0, 0))
    else:
        mod_spec = pl.BlockSpec((1, N_MOD, D_MODEL), lambda i: (MOD_ROWS // 2, 0, 0))
    in_specs = [rows(D_MODEL), rows(512), rows(512), rows(512), _resident((1, DN_DV)), rows(256), rows(256),
                mod_spec, _resident((1, D_MODEL)),
                _resident((D_MODEL, D_MODEL)), _resident((D_MODEL, D_FF)), _resident((D_FF, D_MODEL))]
    args = [x2d, o_fwd, o_bwd, z, dn_g, ob, oc, mod_l, n2g, wp["w_out"], wp["w_ff1"], wp["w_ff2"]]
    if final_g is not None:
        in_specs.append(_resident((1, D_MODEL)))
        args.append(final_g)
    return pl.pallas_call(
        functools.partial(_ffn_kernel, final=final_g is not None),
        grid=(n // tm,), in_specs=in_specs, out_specs=rows(D_MODEL),
        out_shape=jax.ShapeDtypeStruct((n, D_MODEL), F32),
        compiler_params=_params(1), name="ffn",
    )(*args)


def _rope_tables(n_tok, rot_dim, lane0, both_halves=False):
    rows = n_tok // GRID_W
    row = np.repeat(np.arange(rows, dtype=np.float32), GRID_W)
    colp = np.tile(np.arange(GRID_W, dtype=np.float32), rows)
    n_freq = rot_dim // 4
    inv_freq = (np.float32(ROPE_BASE) ** (-np.arange(n_freq, dtype=np.float32) / np.float32(n_freq))).astype(np.float32)
    ang = np.concatenate([row[:, None] * inv_freq, colp[:, None] * inv_freq], axis=-1).astype(np.float32)
    cos = np.cos(ang.astype(np.float64)).astype(np.float32)
    sin = np.sin(ang.astype(np.float64)).astype(np.float32)
    half = rot_dim // 2
    zeros = lambda w: np.zeros((n_tok, w), np.float32)
    tail = LANES - lane0 - rot_dim
    c = np.concatenate([np.ones((n_tok, lane0), np.float32), cos, cos, zeros(tail)], axis=1)
    s1 = np.concatenate([zeros(lane0 + half), sin, zeros(tail)], axis=1)
    s2 = np.concatenate([zeros(lane0), -sin, zeros(half + tail)], axis=1)
    if both_halves:
        c, s1, s2 = (np.concatenate([a[:, :LANES // 2]] * 2, axis=1) for a in (c, s1, s2))
    return jnp.asarray(c), jnp.asarray(s1), jnp.asarray(s2)


def _pad_heads(w, n_heads, width):
    k = w.shape[0]
    w = w.reshape(k, n_heads, width)
    return jnp.pad(w, ((0, 0), (0, 0), (0, LANES - width))).reshape(k, n_heads * LANES)


def _pack_layer(l, w_in, mla_q_norm_g, mla_w_uq, mla_kv_norm_g, mla_w_ukv, w_out, w_ff1, w_ff2):
    w = w_in[l]
    offs = np.cumsum([0, 1536, 512, 4, 4, 4, 4, MLA_Q_LORA, MLA_KV_LORA, MLA_ROPE, 256, 128, 128])
    piece = lambda i: w[:, offs[i]:offs[i + 1]]
    k = w.shape[0]
    small = jnp.concatenate([piece(2), piece(3), piece(4), piece(5), jnp.zeros((k, KR_LANE - 16), F32),
                             piece(8), jnp.zeros((k, LANES - KR_LANE - MLA_ROPE), F32)], axis=1)
    w_packed = jnp.concatenate([piece(0), piece(1), small, piece(6), piece(7),
                                piece(9), piece(10), piece(11)], axis=1)
    ukv = mla_w_ukv[l].reshape(MLA_KV_LORA, MLA_HEADS, MLA_NOPE + MLA_V)
    wuk = jnp.pad(ukv[:, :, :MLA_NOPE], ((0, 0), (0, 0), (0, LANES - MLA_NOPE)))
    wuv_t = ukv[:, :, MLA_NOPE:].reshape(MLA_KV_LORA, MLA_HEADS * MLA_V).T
    return dict(
        w_in=w_packed.astype(BF16),
        gq=mla_q_norm_g[l][None, :], gkv=mla_kv_norm_g[l][None, :],
        wuq=_pad_heads(mla_w_uq[l], MLA_HEADS, MLA_NOPE + MLA_ROPE).astype(BF16),
        wuk=wuk.reshape(MLA_KV_LORA, MLA_HEADS * LANES).astype(BF16),
        wuv=wuv_t.astype(BF16), wsv_t=piece(11).T.astype(BF16),
        w_out=w_out[l].astype(BF16), w_ff1=w_ff1[l].astype(BF16), w_ff2=w_ff2[l].astype(BF16))


def _layer(x2d, mod_l, wp, lp, ctx, final_g, *, batch, seq, latent, tables):
    tm = 512
    outs = _inproj(x2d, mod_l, lp["norm1_g"], wp, tables, latent=latent, seq=seq, tm=tm)
    qkv, z, small, qm, km, vm, qs, ks, vs = outs[:9]
    r3 = lambda a: a.reshape(batch, seq, a.shape[-1])
    prep = _dn_prep(qkv, small, lp["dn_conv_w"], lp["dn_a_log"], lp["dn_dt_bias"], seq=seq)
    o_fwd, o_bwd, state = _dn_scan(prep, ctx["dn"] if latent else None, batch=batch, seq=seq,
                                   want_state=not latent)
    o_b, o_c = _attention(r3(qm), r3(km), vm, r3(qs), r3(ks), vs, ctx if latent else None, lp["swa_sink"],
                          windowed=latent, tq=256)
    n = batch * seq
    x_new = _ffn(x2d, o_fwd, o_bwd, z, lp["dn_norm_g"], o_b.reshape(n, -1), o_c.reshape(n, -1), mod_l,
                 lp["norm2_g"], wp, final_g, latent=latent, seq=seq, tm=tm)
    if latent:
        return x_new, None
    ckv_n, sk, sv = outs[9], outs[10], outs[11]
    new = (state, ckv_n.reshape(batch, seq, MLA_KV_LORA),
           small[:, KR_LANE:KR_LANE + MLA_ROPE].reshape(batch, seq, MLA_ROPE),
           sk.reshape(batch, seq, SWA_KV_HEADS, SWA_HD), sv.reshape(batch, seq, SWA_KV_HEADS, SWA_HD))
    return x_new, new


def kernel(x_prompt, x_sample, state_dn, cache_mla_ckv, cache_mla_krope, cache_swa_k, cache_swa_v,
           c, c_ctx, ada_w, ada_b, norm1_g, norm2_g, w_in, dn_conv_w, dn_a_log, dn_dt_bias, dn_norm_g,
           mla_q_norm_g, mla_w_uq, mla_kv_norm_g, mla_w_ukv, swa_sink, w_out, w_ff1, w_ff2, final_norm_g):
    depth = ada_w.shape[0]
    bp, sp, _ = x_prompt.shape
    bs, ss, _ = x_sample.shape
    p_len = cache_mla_ckv.shape[2]
    assert bs <= MOD_ROWS // 2

    half_rows = MOD_ROWS // 2
    cvec = jnp.concatenate([c, jnp.zeros((half_rows - bs, D_MODEL), F32), c_ctx[None, :],
                            jnp.zeros((half_rows - 1, D_MODEL), F32)], axis=0)
    mod = _modulation(cvec, ada_w, ada_b).reshape(depth, MOD_ROWS, N_MOD, D_MODEL)

    packed = [_pack_layer(l, w_in, mla_q_norm_g, mla_w_uq, mla_kv_norm_g, mla_w_ukv, w_out, w_ff1, w_ff2)
              for l in range(depth)]
    lps = [dict(norm1_g=norm1_g[l][None, :], norm2_g=norm2_g[l][None, :], dn_conv_w=dn_conv_w[l],
                dn_a_log=dn_a_log[l], dn_dt_bias=dn_dt_bias[l], dn_norm_g=dn_norm_g[l][None, :],
                swa_sink=swa_sink[l]) for l in range(depth)]
    final_g = final_norm_g[None, :]

    xp = x_prompt.reshape(bp * sp, D_MODEL)
    news = []
    for l in range(depth):
        xp, new = _layer(xp, mod[l], packed[l], lps[l], None, final_g if l == depth - 1 else None,
                         batch=bp, seq=sp, latent=False, tables=None)
        news.append(new)
    y_prompt = xp.reshape(bp, sp, D_MODEL)

    tables = _rope_tables(ss, MLA_ROPE, KR_LANE) + _rope_tables(ss, SWA_HD, 0, both_halves=True)
    kr_pad = jnp.pad(cache_mla_krope, ((0, 0), (0, 0), (0, 0), (KR_LANE, LANES - KR_LANE - MLA_ROPE)))
    sk_c = cache_swa_k.reshape(bs, depth, p_len, SWA_KV_HEADS * SWA_HD)
    sv_c = cache_swa_v.reshape(bs, depth, p_len, SWA_KV_HEADS * SWA_HD)
    xs = x_sample.reshape(bs * ss, D_MODEL)
    for l in range(depth):
        km, vm, ks, vs = _cache_prep(l, cache_mla_ckv, kr_pad, sk_c, sv_c, packed[l])
        ctx = dict(dn=state_dn[:, l], km=km, vm=vm, ks=ks, vs=vs)
        xs, _ = _layer(xs, mod[l], packed[l], lps[l], ctx, final_g if l == depth - 1 else None,
                       batch=bs, seq=ss, latent=True, tables=tables)
    y_sample = xs.reshape(bs, ss, D_MODEL)

    stack = lambda i: jnp.stack([nw[i] for nw in news], axis=1)
    return (y_prompt, y_sample, stack(0), stack(1), stack(2), stack(3), stack(4))
```
